```python
import jax, jax.numpy as jnp
from jax import lax
import numpy as np

D_MODEL = 4096
BATCH = 4
SEQ = 2048
DEPTH = 2
DEC_BATCH = 128
DEC_SEQ = 4
PAST_LEN = 16384
PAGE_SIZE = 128

N_MIXERS = 2
N_A_LAYERS = (DEPTH + 1) // 2
N_B_LAYERS = DEPTH // 2
N_MEM = 256
MEM_HEADS = 4
MEM_HEAD_DIM = D_MODEL // 16
MEM_DIM = MEM_HEADS * MEM_HEAD_DIM
MIX_DIM = D_MODEL
TOK_DIM = MIX_DIM - MEM_DIM
HGRN_DK = 128
HGRN_HEADS = TOK_DIM // HGRN_DK
HGRN_DV = TOK_DIM // HGRN_HEADS
HGRN_CHUNK = 32
CONV_W = 3
CONV_DIM = TOK_DIM
N_EXPERTS = 32
TOP_K = 4
D_FF = D_MODEL
SWIGLU_LIMIT = 7.0
SWIGLU_ALPHA = 1.702
LN_EPS = 1e-5
RMS_EPS = 1e-6
DEEPNORM_ALPHA = (2.0 * DEPTH) ** 0.25
DEEPNORM_BETA = (8.0 * DEPTH) ** -0.25

kernel_name = 'hgrn2_shortconv_memxattn_moe_deepnorm_step'


def layer_norm(x, g, b):
    xf = x.astype(jnp.float32)
    mu = jnp.mean(xf, axis=-1, keepdims=True)
    var = jnp.mean(jnp.square(xf - mu), axis=-1, keepdims=True)
    return ((xf - mu) * lax.rsqrt(var + LN_EPS) * g + b).astype(x.dtype)


def rms_norm(x, g):
    xf = x.astype(jnp.float32)
    return xf * lax.rsqrt(jnp.mean(xf * xf, axis=-1, keepdims=True) + RMS_EPS) * g


def hgrn_chunk(S, blk):
    q, k, v, g = blk
    C = q.shape[1]
    G = jnp.cumsum(g, axis=1)
    causal = jnp.tril(jnp.ones((C, C), dtype=bool))[None, :, :, None, None]
    decay = jnp.exp(jnp.where(causal, G[:, :, None] - G[:, None, :], -jnp.inf))
    A = jnp.einsum('bthk,bshk,btshk->bhts', q, k, decay)
    o = (jnp.einsum('bhts,bshv->bthv', A, v)
         + jnp.einsum('bthk,bhkv->bthv', q * jnp.exp(G), S))
    GL = G[:, -1]
    S_new = (jnp.exp(GL)[..., None] * S
             + jnp.einsum('bshk,bshv->bhkv', k * jnp.exp(GL[:, None] - G), v))
    return S_new, o


def hgrn_recurrence(q, k, v, g, S0):
    Bn, T = q.shape[:2]
    C = HGRN_CHUNK if T % HGRN_CHUNK == 0 else T
    n = T // C
    to_blocks = lambda a: a.reshape(Bn, n, C, *a.shape[2:]).swapaxes(0, 1)
    S, o = lax.scan(hgrn_chunk, S0, (to_blocks(q), to_blocks(k), to_blocks(v), to_blocks(g)))
    return o.swapaxes(0, 1).reshape(Bn, T, HGRN_HEADS, HGRN_DV), S


def hgrn_mixer(u, S0, lb, gnorm):
    Bn, T, _ = u.shape
    q, f, i, og = jnp.split(u, 4, axis=-1)
    heads = lambda a: a.reshape(Bn, T, HGRN_HEADS, -1).astype(jnp.float32)
    lb = lb.reshape(HGRN_HEADS, HGRN_DK)
    forget = lb + (1.0 - lb) * jax.nn.sigmoid(heads(f))
    o, S = hgrn_recurrence(jax.nn.silu(heads(q)), 1.0 - forget, heads(i), jnp.log(forget),
                           S0.astype(jnp.float32))
    o = rms_norm(o, gnorm) * jax.nn.silu(heads(og))
    return o.reshape(Bn, T, TOK_DIM).astype(u.dtype), S.astype(S0.dtype)


def conv_mixer(u, buf, w):
    Bg, Cg, v = jnp.split(u, 3, axis=-1)
    z = Cg * v
    T = z.shape[1]
    zp = jnp.concatenate([buf.astype(z.dtype), z], axis=1)
    y = w[0] * zp[:, 0:T]
    for j in range(1, CONV_W):
        y = y + w[j] * zp[:, j:j + T]
    return Bg * y, zp[:, T:].astype(buf.dtype)


def mem_attention(qm, mk, mv):
    Bn, T, _ = qm.shape
    q = qm.reshape(Bn, T, MEM_HEADS, MEM_HEAD_DIM)
    s = jnp.einsum('bthd,bmhd->bhtm', q, mk).astype(jnp.float32) * (MEM_HEAD_DIM ** -0.5)
    p = jax.nn.softmax(s, axis=-1).astype(mv.dtype)
    return jnp.einsum('bhtm,bmhd->bthd', p, mv).reshape(Bn, T, MEM_DIM)


def moe(x, w_r, b_r, w1, b1, w2, b2):
    Bn, T, D = x.shape
    xt = x.reshape(Bn * T, D)
    logits = (xt @ w_r + b_r).astype(jnp.float32)
    top_v, top_i = lax.top_k(logits, TOP_K)
    gates = jax.nn.softmax(top_v, axis=-1)
    comb = jnp.einsum('nk,nke->ne', gates,
                      jax.nn.one_hot(top_i, N_EXPERTS, dtype=jnp.float32)).astype(x.dtype)
    y = jnp.zeros_like(xt)
    for e in range(N_EXPERTS):
        gate, up = jnp.split(xt @ w1[e] + b1[e], 2, axis=-1)
        gate = jnp.minimum(gate, SWIGLU_LIMIT)
        up = jnp.clip(up, -SWIGLU_LIMIT, SWIGLU_LIMIT)
        act = (up + 1.0) * gate * jax.nn.sigmoid(SWIGLU_ALPHA * gate)
        y = y + comb[:, e:e + 1] * (act @ w2[e] + b2[e])
    return y.reshape(Bn, T, D)


def run_trunk(x, mem_k, mem_v, hgrn_states, conv_states, w):
    lb_all = jnp.cumsum(jax.nn.softmax(w['hgrn_lb_logits'].astype(jnp.float32), axis=0), axis=0)
    new_h, new_c = [], []
    for l in range(DEPTH):
        j = l // N_MIXERS
        if l % N_MIXERS == 0:
            u = x @ w['w_in_a'][j]
            tok, S = hgrn_mixer(u[..., :4 * TOK_DIM], hgrn_states[j], lb_all[l], w['hgrn_gnorm'][j])
            new_h.append(S)
        else:
            u = x @ w['w_in_b'][j]
            tok, buf = conv_mixer(u[..., :3 * CONV_DIM], conv_states[j], w['conv_w'][j])
            new_c.append(buf)
        mem_out = mem_attention(u[..., -MEM_DIM:], mem_k[l], mem_v[l])
        mix = jnp.concatenate([tok, mem_out], axis=-1) @ w['w_out'][l]
        x = layer_norm(DEEPNORM_ALPHA * x + mix, w['ln1_g'][l], w['ln1_b'][l])
        ffn = moe(x, w['router_w'][l], w['router_b'][l], w['w_gate_up'][l], w['b_gate_up'][l],
                  w['w_down'][l], w['b_down'][l])
        x = layer_norm(DEEPNORM_ALPHA * x + ffn, w['ln2_g'][l], w['ln2_b'][l])
    return x, jnp.stack(new_h), jnp.stack(new_c)


def setup_inputs(seed: int = 0) -> dict:
    key = jax.random.key(seed)
    ks = jax.random.split(key, 24)
    nrm = lambda k, shape, scale: jax.random.normal(k, shape, jnp.float32) * scale
    return {
        'x_prompt': nrm(ks[0], (BATCH, SEQ, D_MODEL), 1.0),
        'x_sample': nrm(ks[1], (DEC_BATCH, DEC_SEQ, D_MODEL), 1.0),
        'mem_prompt': nrm(ks[2], (BATCH, N_MEM, D_MODEL), 1.0),
        'cache_mem_k': nrm(ks[3], (DEPTH, DEC_BATCH, N_MEM, MEM_HEADS, MEM_HEAD_DIM), 1.0),
        'cache_mem_v': nrm(ks[4], (DEPTH, DEC_BATCH, N_MEM, MEM_HEADS, MEM_HEAD_DIM), 1.0),
        'state_hgrn': nrm(ks[5], (N_A_LAYERS, DEC_BATCH, HGRN_HEADS, HGRN_DK, HGRN_DV), 0.1),
        'state_conv': nrm(ks[6], (N_B_LAYERS, DEC_BATCH, CONV_W - 1, CONV_DIM), 1.0),
        'hgrn_lb_logits': nrm(ks[7], (DEPTH + 1, TOK_DIM), 0.1),
        'w_in_a': nrm(ks[8], (N_A_LAYERS, D_MODEL, 4 * TOK_DIM + MEM_DIM), D_MODEL ** -0.5),
        'hgrn_gnorm': 1.0 + nrm(ks[9], (N_A_LAYERS, HGRN_DV), 0.01),
        'w_in_b': nrm(ks[10], (N_B_LAYERS, D_MODEL, 3 * CONV_DIM + MEM_DIM), D_MODEL ** -0.5),
        'conv_w': nrm(ks[11], (N_B_LAYERS, CONV_W, CONV_DIM), CONV_W ** -0.5),
        'w_mem_kv': nrm(ks[12], (DEPTH, D_MODEL, 2 * MEM_DIM), D_MODEL ** -0.5),
        'w_out': nrm(ks[13], (DEPTH, MIX_DIM, D_MODEL), DEEPNORM_BETA * MIX_DIM ** -0.5),
        'ln1_g': 1.0 + nrm(ks[14], (DEPTH, D_MODEL), 0.01),
        'ln1_b': nrm(ks[15], (DEPTH, D_MODEL), 0.01),
        'router_w': nrm(ks[16], (DEPTH, D_MODEL, N_EXPERTS), D_MODEL ** -0.5),
        'router_b': nrm(ks[17], (DEPTH, N_EXPERTS), 0.01),
        'w_gate_up': nrm(ks[18], (DEPTH, N_EXPERTS, D_MODEL, 2 * D_FF), D_MODEL ** -0.5),
        'b_gate_up': nrm(ks[19], (DEPTH, N_EXPERTS, 2 * D_FF), 0.01),
        'w_down': nrm(ks[20], (DEPTH, N_EXPERTS, D_FF, D_MODEL), DEEPNORM_BETA * D_FF ** -0.5),
        'b_down': nrm(ks[21], (DEPTH, N_EXPERTS, D_MODEL), 0.01),
        'ln2_g': 1.0 + nrm(ks[22], (DEPTH, D_MODEL), 0.01),
        'ln2_b': nrm(ks[23], (DEPTH, D_MODEL), 0.01),
    }


def reference(x_prompt, x_sample, mem_prompt, cache_mem_k, cache_mem_v, state_hgrn, state_conv,
              hgrn_lb_logits, w_in_a, hgrn_gnorm, w_in_b, conv_w, w_mem_kv, w_out, ln1_g, ln1_b,
              router_w, router_b, w_gate_up, b_gate_up, w_down, b_down, ln2_g, ln2_b):
    w = dict(hgrn_lb_logits=hgrn_lb_logits, w_in_a=w_in_a, hgrn_gnorm=hgrn_gnorm, w_in_b=w_in_b,
             conv_w=conv_w, w_out=w_out, ln1_g=ln1_g, ln1_b=ln1_b, router_w=router_w,
             router_b=router_b, w_gate_up=w_gate_up, b_gate_up=b_gate_up, w_down=w_down,
             b_down=b_down, ln2_g=ln2_g, ln2_b=ln2_b)
    bp = x_prompt.shape[0]
    kv = jnp.einsum('bmd,lde->lbme', mem_prompt, w_mem_kv)
    new_mem_k_prompt = kv[..., :MEM_DIM].reshape(DEPTH, bp, N_MEM, MEM_HEADS, MEM_HEAD_DIM)
    new_mem_v_prompt = kv[..., MEM_DIM:].reshape(DEPTH, bp, N_MEM, MEM_HEADS, MEM_HEAD_DIM)
    h0 = jnp.zeros((N_A_LAYERS, bp, HGRN_HEADS, HGRN_DK, HGRN_DV), state_hgrn.dtype)
    c0 = jnp.zeros((N_B_LAYERS, bp, CONV_W - 1, CONV_DIM), state_conv.dtype)
    y_prompt, new_state_hgrn_prompt, new_state_conv_prompt = run_trunk(
        x_prompt, new_mem_k_prompt, new_mem_v_prompt, h0, c0, w)
    y_sample, new_state_hgrn_sample, new_state_conv_sample = run_trunk(
        x_sample, cache_mem_k, cache_mem_v, state_hgrn, state_conv, w)
    return (y_prompt, y_sample, new_mem_k_prompt, new_mem_v_prompt, new_state_hgrn_prompt,
            new_state_conv_prompt, new_state_hgrn_sample, new_state_conv_sample)
```

```python
import functools

import jax
import jax.numpy as jnp
from jax import lax
from jax.experimental import pallas as pl
from jax.experimental.pallas import tpu as pltpu

F32 = jnp.float32
BF16 = jnp.bfloat16

DEPTH = 2
HGRN_HEADS = 24
HGRN_DK = 128
HGRN_DV = 128
HGRN_CHUNK = 32
MEM_HEADS = 4
MEM_HEAD_DIM = 256
N_EXPERTS = 32
TOP_K = 4
CONV_W = 3
SWIGLU_LIMIT = 7.0
SWIGLU_ALPHA = 1.702
LN_EPS = 1e-5
RMS_EPS = 1e-6
DEEPNORM_ALPHA = (2.0 * DEPTH) ** 0.25

SUBLANES = 8
LANES = 128
VMEM_LIMIT = 56 * 1024 * 1024

MOE_GROUP_ROWS = 1280
MOE_SUB_ROWS = 256
GATHER_ROWS = 256
COMBINE_TOKENS = 64


def _cparams(n_axes, vmem=VMEM_LIMIT):
    return pltpu.CompilerParams(dimension_semantics=("arbitrary",) * n_axes, vmem_limit_bytes=vmem)


def _sigmoid(x):
    return 1.0 / (1.0 + jnp.exp(-x))


def _mm_kernel(x_ref, w_ref, o_ref, wb_ref):
    @pl.when(pl.program_id(1) == 0)
    def _():
        wb_ref[...] = w_ref[...].astype(BF16)

    o_ref[...] = jnp.dot(x_ref[...], wb_ref[...], preferred_element_type=F32).astype(o_ref.dtype)


def _matmul(x, w3, layer, *, bm, bn, out_dtype, name):
    M, K = x.shape
    N = w3.shape[-1]
    assert M % bm == 0 and N % bn == 0
    return pl.pallas_call(
        _mm_kernel,
        grid=(N // bn, M // bm),
        in_specs=[
            pl.BlockSpec((bm, K), lambda j, i: (i, 0)),
            pl.BlockSpec((None, K, bn), lambda j, i: (layer, 0, j)),
        ],
        out_specs=pl.BlockSpec((bm, bn), lambda j, i: (i, j)),
        out_shape=jax.ShapeDtypeStruct((M, N), out_dtype),
        scratch_shapes=[pltpu.VMEM((K, bn), BF16)],
        compiler_params=_cparams(2),
        name=name,
    )(x, w3)


def _mm2_kernel(a_ref, b_ref, wa_ref, wb_ref, o_ref, wsa_ref, wsb_ref):
    @pl.when(pl.program_id(1) == 0)
    def _():
        wsa_ref[...] = wa_ref[...].astype(BF16)
        wsb_ref[...] = wb_ref[...].astype(BF16)

    acc = jnp.dot(a_ref[...], wsa_ref[...], preferred_element_type=F32)
    acc = acc + jnp.dot(b_ref[...], wsb_ref[...], preferred_element_type=F32)
    o_ref[...] = acc


def _out_proj(tok, mem, w_out, layer, *, bm, bn):
    M, Ka = tok.shape
    Kb = mem.shape[1]
    N = w_out.shape[-1]
    assert Ka % Kb == 0
    return pl.pallas_call(
        _mm2_kernel,
        grid=(N // bn, M // bm),
        in_specs=[
            pl.BlockSpec((bm, Ka), lambda j, i: (i, 0)),
            pl.BlockSpec((bm, Kb), lambda j, i: (i, 0)),
            pl.BlockSpec((None, Ka, bn), lambda j, i: (layer, 0, j)),
            pl.BlockSpec((None, Kb, bn), lambda j, i: (layer, Ka // Kb, j)),
        ],
        out_specs=pl.BlockSpec((bm, bn), lambda j, i: (i, j)),
        out_shape=jax.ShapeDtypeStruct((M, N), F32),
        scratch_shapes=[pltpu.VMEM((Ka, bn), BF16), pltpu.VMEM((Kb, bn), BF16)],
        compiler_params=_cparams(2),
        name="out_proj",
    )(tok, mem, w_out, w_out)


def _hgrn_chunk(uq, uf, ui, uo, lb, gn, st_ref, kpad, gpad, vpad, *, rows, n_valid):
    q = uq * _sigmoid(uq)
    forget = lb + (1.0 - lb) * _sigmoid(uf)
    kk = 1.0 - forget
    g = jnp.log(forget)
    v = ui
    if n_valid < rows:
        row = lax.broadcasted_iota(jnp.int32, (rows, LANES), 0)
        valid = row < n_valid
        kk = jnp.where(valid, kk, 0.0)
        g = jnp.where(valid, g, 0.0)
    G = g
    s = 1
    while s < n_valid:
        gpad[rows:, :] = G
        G = G + gpad[rows - s:2 * rows - s, :]
        s *= 2
    gpad[rows:, :] = G
    kpad[rows:, :] = kk
    vpad[rows:, :] = v
    parts = []
    for j in range(rows // SUBLANES):
        lo = j * SUBLANES
        if lo >= n_valid:
            parts.append(jnp.zeros((SUBLANES, LANES), F32))
            continue
        qj = q[lo:lo + SUBLANES, :]
        Gj = G[lo:lo + SUBLANES, :]
        acc = jnp.zeros((SUBLANES, LANES), F32)
        for d in range(min(lo + SUBLANES, n_valid)):
            a0 = rows + lo - d
            kd = kpad[a0:a0 + SUBLANES, :]
            Gd = gpad[a0:a0 + SUBLANES, :]
            vd = vpad[a0:a0 + SUBLANES, :]
            p = qj * kd * jnp.exp(Gj - Gd)
            acc = acc + jnp.sum(p, axis=-1, keepdims=True) * vd
        parts.append(acc)
    o = parts[0] if len(parts) == 1 else jnp.concatenate(parts, axis=0)
    st = st_ref[...]
    qt = (q * jnp.exp(G)).astype(BF16)
    o = o + lax.dot_general(qt, st.astype(BF16), (((1,), (1,)), ((), ())), preferred_element_type=F32)
    GL = G[n_valid - 1:n_valid, :]
    kt = (kk * jnp.exp(GL - G)).astype(BF16)
    upd = lax.dot_general(v.astype(BF16), kt, (((0,), (0,)), ((), ())), preferred_element_type=F32)
    st_ref[...] = st * jnp.exp(GL) + upd
    ms = jnp.mean(o * o, axis=-1, keepdims=True)
    return o * lax.rsqrt(ms + RMS_EPS) * gn * (uo * _sigmoid(uo))


def _hgrn_prompt_kernel(uq_ref, uf_ref, ui_ref, uo_ref, lb_ref, gn_ref, tok_ref, sout_ref,
                        st_ref, kpad, gpad, vpad, *, tb):
    c = pl.program_id(2)
    C = HGRN_CHUNK

    @pl.when(c == 0)
    def _():
        st_ref[...] = jnp.zeros_like(st_ref)
        kpad[...] = jnp.zeros_like(kpad)
        gpad[...] = jnp.zeros_like(gpad)
        vpad[...] = jnp.zeros_like(vpad)

    lb = lb_ref[...]
    gn = gn_ref[...]

    def chunk(ci, carry):
        r0 = pl.multiple_of(ci * C, C)
        out = _hgrn_chunk(uq_ref[pl.ds(r0, C), :], uf_ref[pl.ds(r0, C), :], ui_ref[pl.ds(r0, C), :],
                          uo_ref[pl.ds(r0, C), :], lb, gn, st_ref, kpad, gpad, vpad, rows=C, n_valid=C)
        tok_ref[pl.ds(r0, C), :] = out.astype(tok_ref.dtype)
        return carry

    lax.fori_loop(0, tb // C, chunk, 0)

    @pl.when(c == pl.num_programs(2) - 1)
    def _():
        sout_ref[...] = st_ref[...].T


def _hgrn_prompt(u, lb, gnorm, *, batch, seq, tb=256):
    H = HGRN_HEADS
    nt = seq // tb
    assert seq % tb == 0 and tb % HGRN_CHUNK == 0
    col = lambda off: (lambda b, h, c: (b * nt + c, off + h))
    return pl.pallas_call(
        functools.partial(_hgrn_prompt_kernel, tb=tb),
        grid=(batch, H, nt),
        in_specs=[
            pl.BlockSpec((tb, LANES), col(0)),
            pl.BlockSpec((tb, LANES), col(H)),
            pl.BlockSpec((tb, LANES), col(2 * H)),
            pl.BlockSpec((tb, LANES), col(3 * H)),
            pl.BlockSpec((None, 1, LANES), lambda b, h, c: (h, 0, 0)),
            pl.BlockSpec((1, LANES), lambda b, h, c: (0, 0)),
        ],
        out_specs=[
            pl.BlockSpec((tb, LANES), lambda b, h, c: (b * nt + c, h)),
            pl.BlockSpec((None, None, HGRN_DK, HGRN_DV), lambda b, h, c: (b, h, 0, 0)),
        ],
        out_shape=[
            jax.ShapeDtypeStruct((batch * seq, H * HGRN_DV), BF16),
            jax.ShapeDtypeStruct((batch, H, HGRN_DK, HGRN_DV), F32),
        ],
        scratch_shapes=[
            pltpu.VMEM((HGRN_DV, HGRN_DK), F32),
            pltpu.VMEM((2 * HGRN_CHUNK, LANES), F32),
            pltpu.VMEM((2 * HGRN_CHUNK, LANES), F32),
            pltpu.VMEM((2 * HGRN_CHUNK, LANES), F32),
        ],
        compiler_params=_cparams(3),
        name="hgrn_prompt",
    )(u, u, u, u, lb.reshape(H, 1, LANES), gnorm.reshape(1, LANES))


def _hgrn_sample_kernel(uq_ref, uf_ref, ui_ref, uo_ref, lb_ref, gn_ref, sin_ref, tok_ref, sout_ref,
                        st_ref, kpad, gpad, vpad, inq, inf, ini, ino, *, bb, seq):
    R = SUBLANES

    @pl.when((pl.program_id(0) == 0) & (pl.program_id(1) == 0))
    def _():
        kpad[...] = jnp.zeros_like(kpad)
        gpad[...] = jnp.zeros_like(gpad)
        vpad[...] = jnp.zeros_like(vpad)
        for r in (inq, inf, ini, ino):
            r[...] = jnp.zeros_like(r)

    lb = lb_ref[...]
    gn = gn_ref[...]
    for b in range(bb):
        inq[0:seq, :] = uq_ref[b]
        inf[0:seq, :] = uf_ref[b]
        ini[0:seq, :] = ui_ref[b]
        ino[0:seq, :] = uo_ref[b]
        st_ref[...] = sin_ref[b].T
        out = _hgrn_chunk(inq[...], inf[...], ini[...], ino[...], lb, gn, st_ref, kpad, gpad, vpad,
                          rows=R, n_valid=seq)
        tok_ref[b] = out[0:seq, :]
        sout_ref[b] = st_ref[...].T


def _hgrn_sample(u3, lb, gnorm, state, *, bb=8):
    B, T, _ = u3.shape
    H = HGRN_HEADS
    assert T <= SUBLANES and B % bb == 0
    col = lambda off: (lambda i, h: (i, 0, off + h))
    sspec = pl.BlockSpec((bb, None, HGRN_DK, HGRN_DV), lambda i, h: (i, h, 0, 0))
    pad = pltpu.VMEM((2 * SUBLANES, LANES), F32)
    row = pltpu.VMEM((SUBLANES, LANES), F32)
    return pl.pallas_call(
        functools.partial(_hgrn_sample_kernel, bb=bb, seq=T),
        grid=(B // bb, H),
        in_specs=[
            pl.BlockSpec((bb, T, LANES), col(0)),
            pl.BlockSpec((bb, T, LANES), col(H)),
            pl.BlockSpec((bb, T, LANES), col(2 * H)),
            pl.BlockSpec((bb, T, LANES), col(3 * H)),
            pl.BlockSpec((None, 1, LANES), lambda i, h: (h, 0, 0)),
            pl.BlockSpec((1, LANES), lambda i, h: (0, 0)),
            sspec,
        ],
        out_specs=[pl.BlockSpec((bb, T, LANES), lambda i, h: (i, 0, h)), sspec],
        out_shape=[
            jax.ShapeDtypeStruct((B, T, H * HGRN_DV), F32),
            jax.ShapeDtypeStruct((B, H, HGRN_DK, HGRN_DV), F32),
        ],
        scratch_shapes=[pltpu.VMEM((HGRN_DV, HGRN_DK), F32), pad, pad, pad, row, row, row, row],
        compiler_params=_cparams(2),
        name="hgrn_sample",
    )(u3, u3, u3, u3, lb.reshape(H, 1, LANES), gnorm.reshape(1, LANES), state)


def _conv_prompt_kernel(bg_ref, cg_ref, v_ref, w_ref, tok_ref, cout_ref, zpad, *, tb):
    t = pl.program_id(2)
    P = SUBLANES

    @pl.when(t == 0)
    def _():
        zpad[0:P, :] = jnp.zeros((P, zpad.shape[1]), F32)

    z = cg_ref[...] * v_ref[...]
    zpad[P:, :] = z
    w = w_ref[...]
    y = w[2:3, :] * z + w[1:2, :] * zpad[P - 1:P - 1 + tb, :] + w[0:1, :] * zpad[P - 2:P - 2 + tb, :]
    tok_ref[...] = (bg_ref[...] * y).astype(tok_ref.dtype)
    zpad[0:P, :] = zpad[tb:tb + P, :]

    @pl.when(t == pl.num_programs(2) - 1)
    def _():
        cout_ref[...] = zpad[P - (CONV_W - 1):P, :]


def _conv_prompt(u, conv_w, *, batch, seq, dim, tb=512, cb=512):
    nt = seq // tb
    nc = dim // cb
    assert seq % tb == 0 and dim % cb == 0
    col = lambda off: (lambda b, c, t: (b * nt + t, off + c))
    return pl.pallas_call(
        functools.partial(_conv_prompt_kernel, tb=tb),
        grid=(batch, nc, nt),
        in_specs=[
            pl.BlockSpec((tb, cb), col(0)),
            pl.BlockSpec((tb, cb), col(nc)),
            pl.BlockSpec((tb, cb), col(2 * nc)),
            pl.BlockSpec((CONV_W, cb), lambda b, c, t: (0, c)),
        ],
        out_specs=[
            pl.BlockSpec((tb, cb), lambda b, c, t: (b * nt + t, c)),
            pl.BlockSpec((None, CONV_W - 1, cb), lambda b, c, t: (b, 0, c)),
        ],
        out_shape=[
            jax.ShapeDtypeStruct((batch * seq, dim), BF16),
            jax.ShapeDtypeStruct((batch, CONV_W - 1, dim), F32),
        ],
        scratch_shapes=[pltpu.VMEM((tb + SUBLANES, cb), F32)],
        compiler_params=_cparams(3),
        name="conv_prompt",
    )(u, u, u, conv_w)


def _conv_sample_kernel(bg_ref, cg_ref, v_ref, w_ref, buf_ref, tok_ref, cout_ref, *, seq):
    w = w_ref[...]
    zp = [buf_ref[:, j, :] for j in range(CONV_W - 1)]
    zp += [cg_ref[:, t, :] * v_ref[:, t, :] for t in range(seq)]
    for t in range(seq):
        y = w[0:1, :] * zp[t]
        for j in range(1, CONV_W):
            y = y + w[j:j + 1, :] * zp[t + j]
        tok_ref[:, t, :] = bg_ref[:, t, :] * y
    for j in range(CONV_W - 1):
        cout_ref[:, j, :] = zp[seq + j]


def _conv_sample(u3, conv_w, buf, *, dim, bb=32, cb=512):
    B, T, _ = u3.shape
    nc = dim // cb
    assert B % bb == 0 and dim % cb == 0
    col = lambda off: (lambda i, c: (i, 0, off + c))
    bspec = pl.BlockSpec((bb, CONV_W - 1, cb), lambda i, c: (i, 0, c))
    return pl.pallas_call(
        functools.partial(_conv_sample_kernel, seq=T),
        grid=(B // bb, nc),
        in_specs=[
            pl.BlockSpec((bb, T, cb), col(0)),
            pl.BlockSpec((bb, T, cb), col(nc)),
            pl.BlockSpec((bb, T, cb), col(2 * nc)),
            pl.BlockSpec((CONV_W, cb), lambda i, c: (0, c)),
            bspec,
        ],
        out_specs=[pl.BlockSpec((bb, T, cb), lambda i, c: (i, 0, c)), bspec],
        out_shape=[
            jax.ShapeDtypeStruct((B, T, dim), F32),
            jax.ShapeDtypeStruct((B, CONV_W - 1, dim), F32),
        ],
        compiler_params=_cparams(2),
        name="conv_sample",
    )(u3, u3, u3, conv_w, buf)


def _attend(q, k, v):
    outs = []
    D = MEM_HEAD_DIM
    for h in range(MEM_HEADS):
        sl = slice(h * D, (h + 1) * D)
        s = lax.dot_general(q[:, sl].astype(BF16), k[:, sl].astype(BF16), (((1,), (1,)), ((), ())),
                            preferred_element_type=F32) * (D ** -0.5)
        m = jnp.max(s, axis=-1, keepdims=True)
        e = jnp.exp(s - m)
        p = e / jnp.sum(e, axis=-1, keepdims=True)
        outs.append(jnp.dot(p.astype(BF16), v[:, sl].astype(BF16), preferred_element_type=F32))
    return jnp.concatenate(outs, axis=-1)


def _attn_prompt_kernel(q_ref, k_ref, v_ref, o_ref):
    o_ref[...] = _attend(q_ref[...], k_ref[...], v_ref[...]).astype(o_ref.dtype)


def _attn_prompt(u, kv, *, batch, seq, n_mem, q_col, tq=512):
    W = MEM_HEADS * MEM_HEAD_DIM
    nt = seq // tq
    return pl.pallas_call(
        _attn_prompt_kernel,
        grid=(batch, nt),
        in_specs=[
            pl.BlockSpec((tq, W), lambda b, t: (b * nt + t, q_col)),
            pl.BlockSpec((n_mem, W), lambda b, t: (b, 0)),
            pl.BlockSpec((n_mem, W), lambda b, t: (b, 1)),
        ],
        out_specs=pl.BlockSpec((tq, W), lambda b, t: (b * nt + t, 0)),
        out_shape=jax.ShapeDtypeStruct((batch * seq, W), BF16),
        compiler_params=_cparams(2),
        name="attn_prompt",
    )(u, kv, kv)


def _attn_sample_kernel(q_ref, k_ref, v_ref, o_ref, qpad, *, bb, seq):
    @pl.when(pl.program_id(0) == 0)
    def _():
        qpad[...] = jnp.zeros_like(qpad)

    for b in range(bb):
        qpad[0:seq, :] = q_ref[b]
        o_ref[b] = _attend(qpad[...], k_ref[b], v_ref[b])[0:seq, :]


def _attn_sample(u3, mem_k, mem_v, layer, *, q_col, bb=4):
    B, T, _ = u3.shape
    n_mem, W = mem_k.shape[2], mem_k.shape[3]
    kspec = pl.BlockSpec((None, bb, n_mem, W), lambda i: (layer, i, 0, 0))
    return pl.pallas_call(
        functools.partial(_attn_sample_kernel, bb=bb, seq=T),
        grid=(B // bb,),
        in_specs=[pl.BlockSpec((bb, T, W), lambda i: (i, 0, q_col)), kspec, kspec],
        out_specs=pl.BlockSpec((bb, T, W), lambda i: (i, 0, 0)),
        out_shape=jax.ShapeDtypeStruct((B, T, W), F32),
        scratch_shapes=[pltpu.VMEM((SUBLANES, W), F32)],
        compiler_params=_cparams(1),
        name="attn_sample",
    )(u3, mem_k, mem_v)


def _layer_norm(h, g, b):
    mu = jnp.mean(h, axis=-1, keepdims=True)
    c = h - mu
    var = jnp.mean(c * c, axis=-1, keepdims=True)
    return c * lax.rsqrt(var + LN_EPS) * g + b


def _ln_router_kernel(x_ref, mix_ref, g_ref, b_ref, rw_ref, rb_ref, y_ref, ti_ref, tg_ref):
    y = _layer_norm(DEEPNORM_ALPHA * x_ref[...] + mix_ref[...], g_ref[...], b_ref[...])
    y_ref[...] = y
    logits = jnp.dot(y, rw_ref[...], preferred_element_type=F32, precision=lax.Precision.HIGHEST)
    logits = logits + rb_ref[...]
    n_e = logits.shape[-1]
    lane_e = lax.broadcasted_iota(jnp.int32, logits.shape, 1)
    lane = lax.broadcasted_iota(jnp.int32, ti_ref.shape, 1)
    vals = logits
    top_v, top_i = [], []
    for _ in range(TOP_K):
        m = jnp.max(vals, axis=-1, keepdims=True)
        idx = jnp.min(jnp.where(vals == m, lane_e, n_e), axis=-1, keepdims=True)
        top_v.append(m)
        top_i.append(idx)
        vals = jnp.where(lane_e == idx, -jnp.inf, vals)
    ex = [jnp.exp(tv - top_v[0]) for tv in top_v]
    den = ex[0]
    for e in ex[1:]:
        den = den + e
    ti = jnp.zeros(ti_ref.shape, jnp.int32)
    tg = jnp.zeros(tg_ref.shape, F32)
    for k in range(TOP_K):
        ti = jnp.where(lane == k, top_i[k], ti)
        tg = jnp.where(lane == k, ex[k] / den, tg)
    ti_ref[...] = ti
    tg_ref[...] = tg


def _ln_router(x, mix, ln_g, ln_b, router_w, router_b, layer, *, tm=256):
    M, D = x.shape
    E = router_w.shape[-1]
    row = pl.BlockSpec((tm, D), lambda i: (i, 0))
    vec = pl.BlockSpec((None, 1, D), lambda i: (layer, 0, 0))
    wide = pl.BlockSpec((tm, LANES), lambda i: (i, 0))
    return pl.pallas_call(
        _ln_router_kernel,
        grid=(M // tm,),
        in_specs=[row, row, vec, vec,
                  pl.BlockSpec((None, D, E), lambda i: (layer, 0, 0)),
                  pl.BlockSpec((None, 1, E), lambda i: (layer, 0, 0))],
        out_specs=[row, wide, wide],
        out_shape=[jax.ShapeDtypeStruct((M, D), F32),
                   jax.ShapeDtypeStruct((M, LANES), jnp.int32),
                   jax.ShapeDtypeStruct((M, LANES), F32)],
        compiler_params=_cparams(1),
        name="ln_router",
    )(x, mix, ln_g.reshape(-1, 1, D), ln_b.reshape(-1, 1, D), router_w, router_b.reshape(-1, 1, E))


def _row_copy(src_hbm, dst, sem, src_row, dst_row):
    return pltpu.make_async_copy(src_hbm.at[pl.ds(src_row, 1)], dst.at[pl.ds(dst_row, 1)], sem)


def _gather_kernel(idx_ref, x_hbm, o_ref, buf, sem, *, rows):
    used = idx_ref[0, 0] >= 0

    @pl.when(used)
    def _():
        def issue(r, c):
            _row_copy(x_hbm, buf, sem, idx_ref[0, r], r).start()
            return c

        lax.fori_loop(0, rows, issue, 0)

        def drain(r, c):
            _row_copy(x_hbm, buf, sem, 0, r).wait()
            return c

        lax.fori_loop(0, rows, drain, 0)
        o_ref[...] = buf[...].astype(o_ref.dtype)

    @pl.when(jnp.logical_not(used))
    def _():
        o_ref[...] = jnp.zeros_like(o_ref)


def _gather_rows(x, idx3, *, rows=GATHER_ROWS):
    n_tiles = idx3.shape[0]
    D = x.shape[1]
    return pl.pallas_call(
        functools.partial(_gather_kernel, rows=rows),
        grid=(n_tiles,),
        in_specs=[
            pl.BlockSpec((None, 1, rows), lambda t: (t, 0, 0), memory_space=pltpu.SMEM),
            pl.BlockSpec(memory_space=pl.ANY),
        ],
        out_specs=pl.BlockSpec((rows, D), lambda t: (t, 0)),
        out_shape=jax.ShapeDtypeStruct((n_tiles * rows, D), BF16),
        scratch_shapes=[pltpu.VMEM((rows, D), F32), pltpu.SemaphoreType.DMA(())],
        compiler_params=_cparams(1),
        name="moe_gather",
    )(idx3, x)


def _moe_up_kernel(ge_ref, gn_ref, gm_ref, x_ref, wg_ref, wu_ref, bg_ref, bu_ref, act_ref, w_scr,
                   *, sub, n_sub, bf):
    g = pl.program_id(0)
    n = gn_ref[g]

    @pl.when(n > 0)
    def _():
        w_scr[:, 0:bf] = wg_ref[...].astype(BF16)
        w_scr[:, bf:2 * bf] = wu_ref[...].astype(BF16)
        bg = bg_ref[...]
        bu = bu_ref[...]

        def body(i, c):
            r0 = pl.multiple_of(i * sub, sub)
            h = jnp.dot(x_ref[pl.ds(r0, sub), :], w_scr[...], preferred_element_type=F32)
            gate = jnp.minimum(h[:, 0:bf] + bg, SWIGLU_LIMIT)
            up = jnp.clip(h[:, bf:2 * bf] + bu, -SWIGLU_LIMIT, SWIGLU_LIMIT)
            act = (up + 1.0) * gate * _sigmoid(SWIGLU_ALPHA * gate)
            act_ref[pl.ds(r0, sub), :] = act.astype(act_ref.dtype)
            return c

        lax.fori_loop(0, n, body, 0)

        def fill(i, c):
            r0 = pl.multiple_of(i * sub, sub)
            act_ref[pl.ds(r0, sub), :] = jnp.zeros((sub, bf), act_ref.dtype)
            return c

        lax.fori_loop(n, n_sub, fill, 0)


def _moe_down_kernel(ge_ref, gn_ref, gm_ref, a_ref, w_ref, b_ref, rw_ref, y_ref, w_scr, *, sub, n_sub, bn):
    g = pl.program_id(0)
    n = gn_ref[g]

    @pl.when(n > 0)
    def _():
        w_scr[...] = w_ref[...].astype(BF16)
        bias = b_ref[...]

        def body(i, c):
            r0 = pl.multiple_of(i * sub, sub)
            y = jnp.dot(a_ref[pl.ds(r0, sub), :], w_scr[...], preferred_element_type=F32) + bias
            y_ref[pl.ds(r0, sub), :] = y * rw_ref[pl.ds(r0, sub), :]
            return c

        lax.fori_loop(0, n, body, 0)

        def fill(i, c):
            r0 = pl.multiple_of(i * sub, sub)
            y_ref[pl.ds(r0, sub), :] = jnp.zeros((sub, bn), F32)
            return c

        lax.fori_loop(n, n_sub, fill, 0)


def _moe_experts(xs, row_w, ge, gn, gm, w_gate_up, b_gate_up, w_down, b_down, layer, *, bf=256, bn=512):
    R, sub = MOE_GROUP_ROWS, MOE_SUB_ROWS
    n_sub = R // sub
    GR, D = xs.shape
    G = GR // R
    F = w_down.shape[2]
    E = w_down.shape[1]
    nj1, nj2 = F // bf, D // bn
    jsel = lambda g, j, gn_ref, last: jnp.where(gn_ref[g] > 0, j, last)

    act = pl.pallas_call(
        functools.partial(_moe_up_kernel, sub=sub, n_sub=n_sub, bf=bf),
        grid_spec=pltpu.PrefetchScalarGridSpec(
            num_scalar_prefetch=3,
            grid=(G, nj1),
            in_specs=[
                pl.BlockSpec((R, D), lambda g, j, ge, gn, gm: (gm[g], 0)),
                pl.BlockSpec((None, None, D, bf),
                             lambda g, j, ge, gn, gm: (layer, ge[g], 0, jsel(g, j, gn, nj1 - 1))),
                pl.BlockSpec((None, None, D, bf),
                             lambda g, j, ge, gn, gm: (layer, ge[g], 0, nj1 + jsel(g, j, gn, nj1 - 1))),
                pl.BlockSpec((None, None, 1, bf),
                             lambda g, j, ge, gn, gm: (layer, ge[g], 0, jsel(g, j, gn, nj1 - 1))),
                pl.BlockSpec((None, None, 1, bf),
                             lambda g, j, ge, gn, gm: (layer, ge[g], 0, nj1 + jsel(g, j, gn, nj1 - 1))),
            ],
            out_specs=pl.BlockSpec((R, bf), lambda g, j, ge, gn, gm: (gm[g], jsel(g, j, gn, nj1 - 1))),
            scratch_shapes=[pltpu.VMEM((D, 2 * bf), BF16)],
        ),
        out_shape=jax.ShapeDtypeStruct((GR, F), BF16),
        compiler_params=_cparams(2),
        name="moe_up",
    )(ge, gn, gm, xs, w_gate_up, w_gate_up, b_gate_up.reshape(-1, E, 1, 2 * F),
      b_gate_up.reshape(-1, E, 1, 2 * F))

    return pl.pallas_call(
        functools.partial(_moe_down_kernel, sub=sub, n_sub=n_sub, bn=bn),
        grid_spec=pltpu.PrefetchScalarGridSpec(
            num_scalar_prefetch=3,
            grid=(G, nj2),
            in_specs=[
                pl.BlockSpec((R, F), lambda g, j, ge, gn, gm: (gm[g], 0)),
                pl.BlockSpec((None, None, F, bn),
                             lambda g, j, ge, gn, gm: (layer, ge[g], 0, jsel(g, j, gn, nj2 - 1))),
                pl.BlockSpec((None, None, 1, bn),
                             lambda g, j, ge, gn, gm: (layer, ge[g], 0, jsel(g, j, gn, nj2 - 1))),
                pl.BlockSpec((R, 1), lambda g, j, ge, gn, gm: (gm[g], 0)),
            ],
            out_specs=pl.BlockSpec((R, bn), lambda g, j, ge, gn, gm: (gm[g], jsel(g, j, gn, nj2 - 1))),
            scratch_shapes=[pltpu.VMEM((F, bn), BF16)],
        ),
        out_shape=jax.ShapeDtypeStruct((GR, D), F32),
        compiler_params=_cparams(2),
        name="moe_down",
    )(ge, gn, gm, act, w_down, b_down.reshape(-1, E, 1, D), row_w)


def _combine_ln_kernel(pos_ref, y_hbm, x_ref, g_ref, b_ref, o_ref, buf, sem, *, tm):
    def issue(r, c):
        for k in range(TOP_K):
            _row_copy(y_hbm, buf.at[k], sem, pos_ref[0, r * TOP_K + k], r).start()
        return c

    lax.fori_loop(0, tm, issue, 0)

    def drain(r, c):
        for k in range(TOP_K):
            _row_copy(y_hbm, buf.at[k], sem, 0, r).wait()
        return c

    lax.fori_loop(0, tm, drain, 0)
    ffn = buf[0]
    for k in range(1, TOP_K):
        ffn = ffn + buf[k]
    o_ref[...] = _layer_norm(DEEPNORM_ALPHA * x_ref[...] + ffn, g_ref[...], b_ref[...])


def _combine_ln(ys, pos3, x, ln_g, ln_b, layer, *, tm=COMBINE_TOKENS):
    M, D = x.shape
    row = pl.BlockSpec((tm, D), lambda i: (i, 0))
    vec = pl.BlockSpec((None, 1, D), lambda i: (layer, 0, 0))
    return pl.pallas_call(
        functools.partial(_combine_ln_kernel, tm=tm),
        grid=(M // tm,),
        in_specs=[
            pl.BlockSpec((None, 1, tm * TOP_K), lambda i: (i, 0, 0), memory_space=pltpu.SMEM),
            pl.BlockSpec(memory_space=pl.ANY),
            row, vec, vec,
        ],
        out_specs=row,
        out_shape=jax.ShapeDtypeStruct((M, D), F32),
        scratch_shapes=[pltpu.VMEM((TOP_K, tm, D), F32), pltpu.SemaphoreType.DMA(())],
        compiler_params=_cparams(1),
        name="moe_combine_ln",
    )(pos3, ys, x, ln_g.reshape(-1, 1, D), ln_b.reshape(-1, 1, D))


def _routing_tables(top_i, top_g, n_groups):
    R, sub = MOE_GROUP_ROWS, MOE_SUB_ROWS
    M = top_i.shape[0]
    e_flat = top_i.reshape(-1)
    onehot = (e_flat[:, None] == jnp.arange(N_EXPERTS, dtype=jnp.int32)[None, :]).astype(jnp.int32)
    csum = jnp.cumsum(onehot, axis=0)
    rank = jnp.take_along_axis(csum, e_flat[:, None], axis=1)[:, 0] - 1
    count = csum[-1]
    groups_e = (count + R - 1) // R
    gend = jnp.cumsum(groups_e)
    gbase = gend - groups_e
    n_used = gend[-1]
    pos = (gbase[e_flat] + rank // R) * R + rank % R
    gid = jnp.arange(n_groups, dtype=jnp.int32)
    gm = jnp.minimum(gid, n_used - 1)
    ge = jnp.searchsorted(gend, gm, side="right").astype(jnp.int32)
    rows_g = jnp.clip(count[ge] - (gm - gbase[ge]) * R, 0, R)
    gn = jnp.where(gid < n_used, (rows_g + sub - 1) // sub, 0).astype(jnp.int32)
    tile_used = (jnp.arange(R // sub, dtype=jnp.int32)[None, :] < gn[:, None]).reshape(-1)
    fill = jnp.repeat(jnp.where(tile_used, 0, -1).astype(jnp.int32), sub)
    tok = jnp.arange(M * TOP_K, dtype=jnp.int32) // TOP_K
    src = fill.at[pos].set(tok)
    row_w = jnp.zeros((n_groups * R,), F32).at[pos].set(top_g.reshape(-1))
    return pos.astype(jnp.int32), src, row_w, ge, gn, gm.astype(jnp.int32)


def _moe_layer(x1, top_i, top_g, w_gate_up, b_gate_up, w_down, b_down, ln_g, ln_b, layer):
    M, D = x1.shape
    R = MOE_GROUP_ROWS
    n_groups = N_EXPERTS + (M * TOP_K) // R
    pos, src, row_w, ge, gn, gm = _routing_tables(top_i, top_g, n_groups)
    xs = _gather_rows(x1, src.reshape(-1, 1, GATHER_ROWS))
    ys = _moe_experts(xs, row_w.reshape(-1, 1), ge, gn, gm, w_gate_up, b_gate_up, w_down, b_down, layer)
    return _combine_ln(ys, pos.reshape(-1, 1, COMBINE_TOKENS * TOP_K), x1, ln_g, ln_b, layer)


def kernel(x_prompt, x_sample, mem_prompt, cache_mem_k, cache_mem_v, state_hgrn, state_conv, hgrn_lb_logits, w_in_a, hgrn_gnorm, w_in_b, conv_w, w_mem_kv, w_out, ln1_g, ln1_b, router_w, router_b, w_gate_up, b_gate_up, w_down, b_down, ln2_g, ln2_b):
    Bp, Tp, D = x_prompt.shape
    Bs, Ts, _ = x_sample.shape
    n_mem = mem_prompt.shape[1]
    MEM = MEM_HEADS * MEM_HEAD_DIM
    TOK = HGRN_HEADS * HGRN_DV
    Mp, Ms = Bp * Tp, Bs * Ts
    M = Mp + Ms
    assert w_in_a.shape[-1] == 4 * TOK + MEM and w_in_b.shape[-1] == 3 * TOK + MEM
    assert w_gate_up.shape[1] == N_EXPERTS and w_out.shape[0] == DEPTH == 2

    memb = mem_prompt.reshape(Bp * n_mem, D).astype(BF16)
    kvs = [_matmul(memb, w_mem_kv, l, bm=Bp * n_mem, bn=512, out_dtype=F32, name="mem_kv") for l in range(DEPTH)]
    kv = jnp.stack(kvs).reshape(DEPTH, Bp, n_mem, 2 * MEM)
    new_mem_k = kv[..., :MEM].reshape(DEPTH, Bp, n_mem, MEM_HEADS, MEM_HEAD_DIM)
    new_mem_v = kv[..., MEM:].reshape(DEPTH, Bp, n_mem, MEM_HEADS, MEM_HEAD_DIM)
    cache_k = cache_mem_k.reshape(DEPTH, Bs, n_mem, MEM)
    cache_v = cache_mem_v.reshape(DEPTH, Bs, n_mem, MEM)

    lb_all = jnp.cumsum(jax.nn.softmax(hgrn_lb_logits.astype(F32), axis=0), axis=0)

    x = jnp.concatenate([x_prompt.reshape(Mp, D), x_sample.reshape(Ms, D)], axis=0)
    bm = M // 8
    for l in range(DEPTH):
        xb = x.astype(BF16)
        if l % 2 == 0:
            u = _matmul(xb, w_in_a, l // 2, bm=bm, bn=512, out_dtype=F32, name="in_proj_a")
            u3 = u[Mp:].reshape(Bs, Ts, -1)
            lb = lb_all[l]
            tok_p, hgrn_p = _hgrn_prompt(u, lb, hgrn_gnorm[l // 2], batch=Bp, seq=Tp)
            tok_s, hgrn_s = _hgrn_sample(u3, lb, hgrn_gnorm[l // 2], state_hgrn[l // 2])
            q_col = 4 * TOK // MEM
        else:
            u = _matmul(xb, w_in_b, l // 2, bm=bm, bn=512, out_dtype=F32, name="in_proj_b")
            u3 = u[Mp:].reshape(Bs, Ts, -1)
            tok_p, conv_p = _conv_prompt(u, conv_w[l // 2], batch=Bp, seq=Tp, dim=TOK)
            tok_s, conv_s = _conv_sample(u3, conv_w[l // 2], state_conv[l // 2], dim=TOK)
            q_col = 3 * TOK // MEM
        mem_p = _attn_prompt(u, kvs[l], batch=Bp, seq=Tp, n_mem=n_mem, q_col=q_col)
        mem_s = _attn_sample(u3, cache_k, cache_v, l, q_col=q_col)
        tok = jnp.concatenate([tok_p, tok_s.reshape(Ms, TOK).astype(BF16)], axis=0)
        mem = jnp.concatenate([mem_p, mem_s.reshape(Ms, MEM).astype(BF16)], axis=0)
        mix = _out_proj(tok, mem, w_out, l, bm=bm, bn=512)
        x1, ti, tg = _ln_router(x, mix, ln1_g, ln1_b, router_w, router_b, l)
        x = _moe_layer(x1, ti[:, :TOP_K], tg[:, :TOP_K], w_gate_up, b_gate_up, w_down, b_down, ln2_g, ln2_b, l)

    y_prompt = x[:Mp].reshape(Bp, Tp, D)
    y_sample = x[Mp:].reshape(Bs, Ts, D)
    return (y_prompt, y_sample, new_mem_k, new_mem_v, hgrn_p[None], conv_p[None], hgrn_s[None], conv_s[None])
```

```python
import functools

import jax
import jax.numpy as jnp
from jax import lax
from jax.experimental import pallas as pl
from jax.experimental.pallas import tpu as pltpu

F32 = jnp.float32
BF16 = jnp.bfloat16

DEPTH = 2
HGRN_HEADS = 24
HGRN_DK = 128
HGRN_DV = 128
HGRN_CHUNK = 32
MEM_HEADS = 4
MEM_HEAD_DIM = 256
N_EXPERTS = 32
TOP_K = 4
CONV_W = 3
SWIGLU_LIMIT = 7.0
SWIGLU_ALPHA = 1.702
LN_EPS = 1e-5
RMS_EPS = 1e-6
DEEPNORM_ALPHA = (2.0 * DEPTH) ** 0.25

SUBLANES = 8
LANES = 128
VMEM_LIMIT = 56 * 1024 * 1024

MOE_GROUP_ROWS = 1280
MOE_SUB_ROWS = 128
MOE_UP_VMEM_LIMIT = 58 * 1024 * 1024
COMBINE_TOKENS = 64


def _cparams(n_axes, vmem=VMEM_LIMIT):
    return pltpu.CompilerParams(dimension_semantics=("arbitrary",) * n_axes, vmem_limit_bytes=vmem)


def _sigmoid(x):
    return 1.0 / (1.0 + jnp.exp(-x))


def _mm_kernel(x_ref, w_ref, o_ref, wb_ref):
    @pl.when(pl.program_id(1) == 0)
    def _():
        wb_ref[...] = w_ref[...].astype(BF16)

    o_ref[...] = jnp.dot(x_ref[...], wb_ref[...], preferred_element_type=F32).astype(o_ref.dtype)


def _matmul(x, w3, layer, *, bm, bn, out_dtype, name):
    M, K = x.shape
    N = w3.shape[-1]
    assert M % bm == 0 and N % bn == 0
    return pl.pallas_call(
        _mm_kernel,
        grid=(N // bn, M // bm),
        in_specs=[
            pl.BlockSpec((bm, K), lambda j, i: (i, 0)),
            pl.BlockSpec((None, K, bn), lambda j, i: (layer, 0, j)),
        ],
        out_specs=pl.BlockSpec((bm, bn), lambda j, i: (i, j)),
        out_shape=jax.ShapeDtypeStruct((M, N), out_dtype),
        scratch_shapes=[pltpu.VMEM((K, bn), BF16)],
        compiler_params=_cparams(2),
        name=name,
    )(x, w3)


def _mm2_kernel(a_ref, b_ref, wa_ref, wb_ref, o_ref, wsa_ref, wsb_ref):
    @pl.when(pl.program_id(1) == 0)
    def _():
        wsa_ref[...] = wa_ref[...].astype(BF16)
        wsb_ref[...] = wb_ref[...].astype(BF16)

    acc = jnp.dot(a_ref[...], wsa_ref[...], preferred_element_type=F32)
    acc = acc + jnp.dot(b_ref[...], wsb_ref[...], preferred_element_type=F32)
    o_ref[...] = acc


def _out_proj(tok, mem, w_out, layer, *, bm, bn):
    M, Ka = tok.shape
    Kb = mem.shape[1]
    N = w_out.shape[-1]
    assert Ka % Kb == 0
    return pl.pallas_call(
        _mm2_kernel,
        grid=(N // bn, M // bm),
        in_specs=[
            pl.BlockSpec((bm, Ka), lambda j, i: (i, 0)),
            pl.BlockSpec((bm, Kb), lambda j, i: (i, 0)),
            pl.BlockSpec((None, Ka, bn), lambda j, i: (layer, 0, j)),
            pl.BlockSpec((None, Kb, bn), lambda j, i: (layer, Ka // Kb, j)),
        ],
        out_specs=pl.BlockSpec((bm, bn), lambda j, i: (i, j)),
        out_shape=jax.ShapeDtypeStruct((M, N), F32),
        scratch_shapes=[pltpu.VMEM((Ka, bn), BF16), pltpu.VMEM((Kb, bn), BF16)],
        compiler_params=_cparams(2),
        name="out_proj",
    )(tok, mem, w_out, w_out)


def _hgrn_chunk(uq, uf, ui, uo, lb, gn, st_ref, kpad, gpad, vpad, *, rows, n_valid):
    q = uq * _sigmoid(uq)
    forget = lb + (1.0 - lb) * _sigmoid(uf)
    kk = 1.0 - forget
    g = jnp.log(forget)
    v = ui
    if n_valid < rows:
        row = lax.broadcasted_iota(jnp.int32, (rows, LANES), 0)
        valid = row < n_valid
        kk = jnp.where(valid, kk, 0.0)
        g = jnp.where(valid, g, 0.0)
    G = g
    s = 1
    while s < n_valid:
        gpad[rows:, :] = G
        G = G + gpad[rows - s:2 * rows - s, :]
        s *= 2
    gpad[rows:, :] = G
    kpad[rows:, :] = kk
    vpad[rows:, :] = v
    parts = []
    for j in range(rows // SUBLANES):
        lo = j * SUBLANES
        if lo >= n_valid:
            parts.append(jnp.zeros((SUBLANES, LANES), F32))
            continue
        qj = q[lo:lo + SUBLANES, :]
        Gj = G[lo:lo + SUBLANES, :]
        acc = jnp.zeros((SUBLANES, LANES), F32)
        for d in range(min(lo + SUBLANES, n_valid)):
            a0 = rows + lo - d
            kd = kpad[a0:a0 + SUBLANES, :]
            Gd = gpad[a0:a0 + SUBLANES, :]
            vd = vpad[a0:a0 + SUBLANES, :]
            p = qj * kd * jnp.exp(Gj - Gd)
            acc = acc + jnp.sum(p, axis=-1, keepdims=True) * vd
        parts.append(acc)
    o = parts[0] if len(parts) == 1 else jnp.concatenate(parts, axis=0)
    st = st_ref[...]
    qt = (q * jnp.exp(G)).astype(BF16)
    o = o + lax.dot_general(qt, st.astype(BF16), (((1,), (1,)), ((), ())), preferred_element_type=F32)
    GL = G[n_valid - 1:n_valid, :]
    kt = (kk * jnp.exp(GL - G)).astype(BF16)
    upd = lax.dot_general(v.astype(BF16), kt, (((0,), (0,)), ((), ())), preferred_element_type=F32)
    st_ref[...] = st * jnp.exp(GL) + upd
    ms = jnp.mean(o * o, axis=-1, keepdims=True)
    return o * lax.rsqrt(ms + RMS_EPS) * gn * (uo * _sigmoid(uo))


def _hgrn_prompt_kernel(uq_ref, uf_ref, ui_ref, uo_ref, lb_ref, gn_ref, tok_ref, sout_ref,
                        st_ref, kpad, gpad, vpad, *, tb, hb):
    c = pl.program_id(2)
    C = HGRN_CHUNK

    @pl.when(c == 0)
    def _():
        st_ref[...] = jnp.zeros_like(st_ref)
        kpad[...] = jnp.zeros_like(kpad)
        gpad[...] = jnp.zeros_like(gpad)
        vpad[...] = jnp.zeros_like(vpad)

    gn = gn_ref[...]

    def chunk(ci, carry):
        r0 = pl.multiple_of(ci * C, C)
        for i in range(hb):
            cols = slice(i * LANES, (i + 1) * LANES)
            out = _hgrn_chunk(uq_ref[pl.ds(r0, C), cols], uf_ref[pl.ds(r0, C), cols], ui_ref[pl.ds(r0, C), cols],
                              uo_ref[pl.ds(r0, C), cols], lb_ref[:, cols], gn, st_ref.at[i], kpad.at[i],
                              gpad.at[i], vpad.at[i], rows=C, n_valid=C)
            tok_ref[pl.ds(r0, C), cols] = out.astype(tok_ref.dtype)
        return carry

    lax.fori_loop(0, tb // C, chunk, 0)

    @pl.when(c == pl.num_programs(2) - 1)
    def _():
        for i in range(hb):
            sout_ref[i] = st_ref[i].T


def _hgrn_prompt(u, lb, gnorm, *, batch, seq, tb=256, hb=4):
    H = HGRN_HEADS
    nt = seq // tb
    nh = H // hb
    W = hb * LANES
    assert seq % tb == 0 and tb % HGRN_CHUNK == 0 and H % hb == 0
    col = lambda off: (lambda b, h, c: (b * nt + c, off + h))
    pad = pltpu.VMEM((hb, 2 * HGRN_CHUNK, LANES), F32)
    return pl.pallas_call(
        functools.partial(_hgrn_prompt_kernel, tb=tb, hb=hb),
        grid=(batch, nh, nt),
        in_specs=[
            pl.BlockSpec((tb, W), col(0)),
            pl.BlockSpec((tb, W), col(nh)),
            pl.BlockSpec((tb, W), col(2 * nh)),
            pl.BlockSpec((tb, W), col(3 * nh)),
            pl.BlockSpec((None, 1, W), lambda b, h, c: (h, 0, 0)),
            pl.BlockSpec((1, LANES), lambda b, h, c: (0, 0)),
        ],
        out_specs=[
            pl.BlockSpec((tb, W), lambda b, h, c: (b * nt + c, h)),
            pl.BlockSpec((None, hb, HGRN_DK, HGRN_DV), lambda b, h, c: (b, h, 0, 0)),
        ],
        out_shape=[
            jax.ShapeDtypeStruct((batch * seq, H * HGRN_DV), BF16),
            jax.ShapeDtypeStruct((batch, H, HGRN_DK, HGRN_DV), F32),
        ],
        scratch_shapes=[pltpu.VMEM((hb, HGRN_DV, HGRN_DK), F32), pad, pad, pad],
        compiler_params=_cparams(3),
        name="hgrn_prompt",
    )(u, u, u, u, lb.reshape(nh, 1, W), gnorm.reshape(1, LANES))


def _hgrn_sample_kernel(uq_ref, uf_ref, ui_ref, uo_ref, lb_ref, gn_ref, sin_ref, tok_ref, sout_ref,
                        st_ref, kpad, gpad, vpad, inq, inf, ini, ino, *, bb, seq):
    R = SUBLANES

    @pl.when((pl.program_id(0) == 0) & (pl.program_id(1) == 0))
    def _():
        kpad[...] = jnp.zeros_like(kpad)
        gpad[...] = jnp.zeros_like(gpad)
        vpad[...] = jnp.zeros_like(vpad)
        for r in (inq, inf, ini, ino):
            r[...] = jnp.zeros_like(r)

    lb = lb_ref[...]
    gn = gn_ref[...]
    for b in range(bb):
        inq[0:seq, :] = uq_ref[b]
        inf[0:seq, :] = uf_ref[b]
        ini[0:seq, :] = ui_ref[b]
        ino[0:seq, :] = uo_ref[b]
        st_ref[...] = sin_ref[b].T
        out = _hgrn_chunk(inq[...], inf[...], ini[...], ino[...], lb, gn, st_ref, kpad, gpad, vpad,
                          rows=R, n_valid=seq)
        tok_ref[b] = out[0:seq, :]
        sout_ref[b] = st_ref[...].T


def _hgrn_sample(u3, lb, gnorm, state, *, bb=8):
    B, T, _ = u3.shape
    H = HGRN_HEADS
    assert T <= SUBLANES and B % bb == 0
    col = lambda off: (lambda i, h: (i, 0, off + h))
    sspec = pl.BlockSpec((bb, None, HGRN_DK, HGRN_DV), lambda i, h: (i, h, 0, 0))
    pad = pltpu.VMEM((2 * SUBLANES, LANES), F32)
    row = pltpu.VMEM((SUBLANES, LANES), F32)
    return pl.pallas_call(
        functools.partial(_hgrn_sample_kernel, bb=bb, seq=T),
        grid=(B // bb, H),
        in_specs=[
            pl.BlockSpec((bb, T, LANES), col(0)),
            pl.BlockSpec((bb, T, LANES), col(H)),
            pl.BlockSpec((bb, T, LANES), col(2 * H)),
            pl.BlockSpec((bb, T, LANES), col(3 * H)),
            pl.BlockSpec((None, 1, LANES), lambda i, h: (h, 0, 0)),
            pl.BlockSpec((1, LANES), lambda i, h: (0, 0)),
            sspec,
        ],
        out_specs=[pl.BlockSpec((bb, T, LANES), lambda i, h: (i, 0, h)), sspec],
        out_shape=[
            jax.ShapeDtypeStruct((B, T, H * HGRN_DV), F32),
            jax.ShapeDtypeStruct((B, H, HGRN_DK, HGRN_DV), F32),
        ],
        scratch_shapes=[pltpu.VMEM((HGRN_DV, HGRN_DK), F32), pad, pad, pad, row, row, row, row],
        compiler_params=_cparams(2),
        name="hgrn_sample",
    )(u3, u3, u3, u3, lb.reshape(H, 1, LANES), gnorm.reshape(1, LANES), state)


def _conv_prompt_kernel(bg_ref, cg_ref, v_ref, w_ref, tok_ref, cout_ref, zpad, *, tb):
    t = pl.program_id(2)
    P = SUBLANES

    @pl.when(t == 0)
    def _():
        zpad[0:P, :] = jnp.zeros((P, zpad.shape[1]), F32)

    z = cg_ref[...] * v_ref[...]
    zpad[P:, :] = z
    w = w_ref[...]
    y = w[2:3, :] * z + w[1:2, :] * zpad[P - 1:P - 1 + tb, :] + w[0:1, :] * zpad[P - 2:P - 2 + tb, :]
    tok_ref[...] = (bg_ref[...] * y).astype(tok_ref.dtype)
    zpad[0:P, :] = zpad[tb:tb + P, :]

    @pl.when(t == pl.num_programs(2) - 1)
    def _():
        cout_ref[...] = zpad[P - (CONV_W - 1):P, :]


def _conv_prompt(u, conv_w, *, batch, seq, dim, tb=512, cb=512):
    nt = seq // tb
    nc = dim // cb
    assert seq % tb == 0 and dim % cb == 0
    col = lambda off: (lambda b, c, t: (b * nt + t, off + c))
    return pl.pallas_call(
        functools.partial(_conv_prompt_kernel, tb=tb),
        grid=(batch, nc, nt),
        in_specs=[
            pl.BlockSpec((tb, cb), col(0)),
            pl.BlockSpec((tb, cb), col(nc)),
            pl.BlockSpec((tb, cb), col(2 * nc)),
            pl.BlockSpec((CONV_W, cb), lambda b, c, t: (0, c)),
        ],
        out_specs=[
            pl.BlockSpec((tb, cb), lambda b, c, t: (b * nt + t, c)),
            pl.BlockSpec((None, CONV_W - 1, cb), lambda b, c, t: (b, 0, c)),
        ],
        out_shape=[
            jax.ShapeDtypeStruct((batch * seq, dim), BF16),
            jax.ShapeDtypeStruct((batch, CONV_W - 1, dim), F32),
        ],
        scratch_shapes=[pltpu.VMEM((tb + SUBLANES, cb), F32)],
        compiler_params=_cparams(3),
        name="conv_prompt",
    )(u, u, u, conv_w)


def _conv_sample_kernel(bg_ref, cg_ref, v_ref, w_ref, buf_ref, tok_ref, cout_ref, *, seq):
    w = w_ref[...]
    zp = [buf_ref[:, j, :] for j in range(CONV_W - 1)]
    zp += [cg_ref[:, t, :] * v_ref[:, t, :] for t in range(seq)]
    for t in range(seq):
        y = w[0:1, :] * zp[t]
        for j in range(1, CONV_W):
            y = y + w[j:j + 1, :] * zp[t + j]
        tok_ref[:, t, :] = bg_ref[:, t, :] * y
    for j in range(CONV_W - 1):
        cout_ref[:, j, :] = zp[seq + j]


def _conv_sample(u3, conv_w, buf, *, dim, bb=32, cb=512):
    B, T, _ = u3.shape
    nc = dim // cb
    assert B % bb == 0 and dim % cb == 0
    col = lambda off: (lambda i, c: (i, 0, off + c))
    bspec = pl.BlockSpec((bb, CONV_W - 1, cb), lambda i, c: (i, 0, c))
    return pl.pallas_call(
        functools.partial(_conv_sample_kernel, seq=T),
        grid=(B // bb, nc),
        in_specs=[
            pl.BlockSpec((bb, T, cb), col(0)),
            pl.BlockSpec((bb, T, cb), col(nc)),
            pl.BlockSpec((bb, T, cb), col(2 * nc)),
            pl.BlockSpec((CONV_W, cb), lambda i, c: (0, c)),
            bspec,
        ],
        out_specs=[pl.BlockSpec((bb, T, cb), lambda i, c: (i, 0, c)), bspec],
        out_shape=[
            jax.ShapeDtypeStruct((B, T, dim), F32),
            jax.ShapeDtypeStruct((B, CONV_W - 1, dim), F32),
        ],
        compiler_params=_cparams(2),
        name="conv_sample",
    )(u3, u3, u3, conv_w, buf)


def _attend(q, k, v):
    outs = []
    D = MEM_HEAD_DIM
    for h in range(MEM_HEADS):
        sl = slice(h * D, (h + 1) * D)
        s = lax.dot_general(q[:, sl].astype(BF16), k[:, sl].astype(BF16), (((1,), (1,)), ((), ())),
                            preferred_element_type=F32) * (D ** -0.5)
        m = jnp.max(s, axis=-1, keepdims=True)
        e = jnp.exp(s - m)
        p = e / jnp.sum(e, axis=-1, keepdims=True)
        outs.append(jnp.dot(p.astype(BF16), v[:, sl].astype(BF16), preferred_element_type=F32))
    return jnp.concatenate(outs, axis=-1)


def _attn_prompt_kernel(q_ref, k_ref, v_ref, o_ref):
    o_ref[...] = _attend(q_ref[...], k_ref[...], v_ref[...]).astype(o_ref.dtype)


def _attn_prompt(u, kv, *, batch, seq, n_mem, q_col, tq=512):
    W = MEM_HEADS * MEM_HEAD_DIM
    nt = seq // tq
    return pl.pallas_call(
        _attn_prompt_kernel,
        grid=(batch, nt),
        in_specs=[
            pl.BlockSpec((tq, W), lambda b, t: (b * nt + t, q_col)),
            pl.BlockSpec((n_mem, W), lambda b, t: (b, 0)),
            pl.BlockSpec((n_mem, W), lambda b, t: (b, 1)),
        ],
        out_specs=pl.BlockSpec((tq, W), lambda b, t: (b * nt + t, 0)),
        out_shape=jax.ShapeDtypeStruct((batch * seq, W), BF16),
        compiler_params=_cparams(2),
        name="attn_prompt",
    )(u, kv, kv)


def _attn_sample_kernel(q_ref, k_ref, v_ref, o_ref, qpad, *, bb, seq):
    @pl.when(pl.program_id(0) == 0)
    def _():
        qpad[...] = jnp.zeros_like(qpad)

    for b in range(bb):
        qpad[0:seq, :] = q_ref[b]
        o_ref[b] = _attend(qpad[...], k_ref[b], v_ref[b])[0:seq, :]


def _attn_sample(u3, mem_k, mem_v, layer, *, q_col, bb=4):
    B, T, _ = u3.shape
    n_mem, W = mem_k.shape[2], mem_k.shape[3]
    kspec = pl.BlockSpec((None, bb, n_mem, W), lambda i: (layer, i, 0, 0))
    return pl.pallas_call(
        functools.partial(_attn_sample_kernel, bb=bb, seq=T),
        grid=(B // bb,),
        in_specs=[pl.BlockSpec((bb, T, W), lambda i: (i, 0, q_col)), kspec, kspec],
        out_specs=pl.BlockSpec((bb, T, W), lambda i: (i, 0, 0)),
        out_shape=jax.ShapeDtypeStruct((B, T, W), F32),
        scratch_shapes=[pltpu.VMEM((SUBLANES, W), F32)],
        compiler_params=_cparams(1),
        name="attn_sample",
    )(u3, mem_k, mem_v)


def _layer_norm(h, g, b):
    mu = jnp.mean(h, axis=-1, keepdims=True)
    c = h - mu
    var = jnp.mean(c * c, axis=-1, keepdims=True)
    return c * lax.rsqrt(var + LN_EPS) * g + b


def _pack_bf16_pairs(y):
    w = y.shape[1] // 2
    bits = pltpu.bitcast(y.astype(BF16).astype(F32), jnp.uint32)
    return (bits[:, w:] & jnp.uint32(0xFFFF0000)) | (bits[:, :w] >> 16)


def _unpack_bf16_pairs(p):
    lo = pltpu.bitcast(p << 16, F32).astype(BF16)
    hi = pltpu.bitcast(p & jnp.uint32(0xFFFF0000), F32).astype(BF16)
    return lo, hi


def _ln_router_kernel(x_ref, mix_ref, g_ref, b_ref, rw_ref, rb_ref, y_ref, yp_ref, ti_ref, tg_ref):
    y = _layer_norm(DEEPNORM_ALPHA * x_ref[...] + mix_ref[...], g_ref[...], b_ref[...])
    y_ref[...] = y
    yp_ref[...] = _pack_bf16_pairs(y)
    logits = jnp.dot(y, rw_ref[...], preferred_element_type=F32, precision=lax.Precision.HIGHEST)
    logits = logits + rb_ref[...]
    n_e = logits.shape[-1]
    lane_e = lax.broadcasted_iota(jnp.int32, logits.shape, 1)
    lane = lax.broadcasted_iota(jnp.int32, ti_ref.shape, 1)
    vals = logits
    top_v, top_i = [], []
    for _ in range(TOP_K):
        m = jnp.max(vals, axis=-1, keepdims=True)
        idx = jnp.min(jnp.where(vals == m, lane_e, n_e), axis=-1, keepdims=True)
        top_v.append(m)
        top_i.append(idx)
        vals = jnp.where(lane_e == idx, -jnp.inf, vals)
    ex = [jnp.exp(tv - top_v[0]) for tv in top_v]
    den = ex[0]
    for e in ex[1:]:
        den = den + e
    ti = jnp.zeros(ti_ref.shape, jnp.int32)
    tg = jnp.zeros(tg_ref.shape, F32)
    for k in range(TOP_K):
        ti = jnp.where(lane == k, top_i[k], ti)
        tg = jnp.where(lane == k, ex[k] / den, tg)
    ti_ref[...] = ti
    tg_ref[...] = tg


def _ln_router(x, mix, ln_g, ln_b, router_w, router_b, layer, *, tm=256):
    M, D = x.shape
    E = router_w.shape[-1]
    row = pl.BlockSpec((tm, D), lambda i: (i, 0))
    half = pl.BlockSpec((tm, D // 2), lambda i: (i, 0))
    vec = pl.BlockSpec((None, 1, D), lambda i: (layer, 0, 0))
    wide = pl.BlockSpec((tm, LANES), lambda i: (i, 0))
    return pl.pallas_call(
        _ln_router_kernel,
        grid=(M // tm,),
        in_specs=[row, row, vec, vec,
                  pl.BlockSpec((None, D, E), lambda i: (layer, 0, 0)),
                  pl.BlockSpec((None, 1, E), lambda i: (layer, 0, 0))],
        out_specs=[row, half, wide, wide],
        out_shape=[jax.ShapeDtypeStruct((M, D), F32),
                   jax.ShapeDtypeStruct((M, D // 2), jnp.uint32),
                   jax.ShapeDtypeStruct((M, LANES), jnp.int32),
                   jax.ShapeDtypeStruct((M, LANES), F32)],
        compiler_params=_cparams(1),
        name="ln_router",
    )(x, mix, ln_g.reshape(-1, 1, D), ln_b.reshape(-1, 1, D), router_w, router_b.reshape(-1, 1, E))


def _row_copy(src_hbm, dst, sem, src_row, dst_row):
    return pltpu.make_async_copy(src_hbm.at[pl.ds(src_row, 1)], dst.at[pl.ds(dst_row, 1)], sem)


def _row_tiles(n_sub, tile_fn):
    sub = MOE_SUB_ROWS
    n_big = n_sub // 4

    def big(i, c):
        tile_fn(pl.multiple_of(i * (4 * sub), 4 * sub), 4 * sub)
        return c

    lax.fori_loop(0, n_big, big, 0)
    rem = n_sub - 4 * n_big
    base = n_big * (4 * sub)

    @pl.when(rem >= 2)
    def _():
        tile_fn(pl.multiple_of(base, sub), 2 * sub)

    @pl.when(rem % 2 == 1)
    def _():
        tile_fn(pl.multiple_of(base + jnp.where(rem >= 2, 2 * sub, 0), sub), sub)


def _zero_unused_rows(ref, n_sub, total_sub):
    sub = MOE_SUB_ROWS

    def fill(i, c):
        ref[pl.ds(pl.multiple_of(i * sub, sub), sub), :] = jnp.zeros((sub, ref.shape[1]), ref.dtype)
        return c

    lax.fori_loop(n_sub, total_sub, fill, 0)


def _moe_up_kernel(ge_ref, gn_ref, gm_ref, gr_ref, src_ref, srcn_ref, xp_hbm, wg_ref, wu_ref, bg_ref, bu_ref,
                   act_ref, xraw, xb, w_scr, sem, *, bf, chunk):
    g = pl.program_id(0)
    j = pl.program_id(1)
    n = gn_ref[g]
    slot = g % 2
    sub = MOE_SUB_ROWS
    half = xraw.shape[2]

    def fetch(idx_ref, dst_slot, lo, hi):
        def body(r, c):
            pltpu.make_async_copy(xp_hbm.at[pl.ds(idx_ref[0, r], 1)], xraw.at[dst_slot, pl.ds(r, 1)],
                                  sem.at[dst_slot]).start()
            return c

        lax.fori_loop(lo, hi, body, 0)

    @pl.when((g == 0) & (j == 0))
    def _():
        fetch(src_ref, 0, 0, gr_ref[0])

    @pl.when((n > 0) & (j == 0))
    def _():
        rows = gr_ref[g]

        def drain(r, c):
            pltpu.make_async_copy(xp_hbm.at[pl.ds(0, 1)], xraw.at[slot, pl.ds(0, 1)], sem.at[slot]).wait()
            return c

        lax.fori_loop(0, rows, drain, 0)

        def unpack(t, c):
            r0 = pl.multiple_of(t * sub, sub)
            p = xraw[slot, pl.ds(r0, sub), :]
            row = lax.broadcasted_iota(jnp.int32, p.shape, 0) + r0
            lo, hi = _unpack_bf16_pairs(jnp.where(row < rows, p, jnp.uint32(0)))
            xb[pl.ds(r0, sub), 0:half] = lo
            xb[pl.ds(r0, sub), half:2 * half] = hi
            return c

        lax.fori_loop(0, n, unpack, 0)

    g_next = jnp.minimum(g + 1, pl.num_programs(0) - 1)

    @pl.when((n > 0) & (g_next > g) & (gn_ref[g_next] > 0))
    def _():
        rows_next = gr_ref[g_next]
        fetch(srcn_ref, 1 - slot, jnp.minimum(j * chunk, rows_next), jnp.minimum((j + 1) * chunk, rows_next))

    @pl.when(n > 0)
    def _():
        w_scr[:, 0:bf] = wg_ref[...].astype(BF16)
        w_scr[:, bf:2 * bf] = wu_ref[...].astype(BF16)
        bg = bg_ref[...]
        bu = bu_ref[...]

        def tile(r0, rows):
            h = jnp.dot(xb[pl.ds(r0, rows), :], w_scr[...], preferred_element_type=F32)
            gate = jnp.minimum(h[:, 0:bf] + bg, SWIGLU_LIMIT)
            up = jnp.clip(h[:, bf:2 * bf] + bu, -SWIGLU_LIMIT, SWIGLU_LIMIT)
            act = (up + 1.0) * gate * _sigmoid(SWIGLU_ALPHA * gate)
            act_ref[pl.ds(r0, rows), :] = act.astype(act_ref.dtype)

        _row_tiles(n, tile)
        _zero_unused_rows(act_ref, n, act_ref.shape[0] // sub)


def _moe_down_kernel(ge_ref, gn_ref, gm_ref, gr_ref, a_ref, w_ref, b_ref, y_ref, w_scr):
    g = pl.program_id(0)
    n = gn_ref[g]

    @pl.when(n > 0)
    def _():
        w_scr[...] = w_ref[...].astype(BF16)
        bias = b_ref[...]

        def tile(r0, rows):
            y_ref[pl.ds(r0, rows), :] = jnp.dot(a_ref[pl.ds(r0, rows), :], w_scr[...],
                                                preferred_element_type=F32) + bias

        _row_tiles(n, tile)
        _zero_unused_rows(y_ref, n, y_ref.shape[0] // MOE_SUB_ROWS)


def _moe_experts(xp, src3, ge, gn, gm, gr, w_gate_up, b_gate_up, w_down, b_down, layer, *, bf=256, bn=512):
    R = MOE_GROUP_ROWS
    G = src3.shape[0]
    GR = G * R
    D = 2 * xp.shape[1]
    F = w_down.shape[2]
    E = w_down.shape[1]
    nj1, nj2 = F // bf, D // bn
    jsel = lambda g, j, gn_ref, last: jnp.where(gn_ref[g] > 0, j, last)
    chunk = -(-R // nj1)

    act = pl.pallas_call(
        functools.partial(_moe_up_kernel, bf=bf, chunk=chunk),
        grid_spec=pltpu.PrefetchScalarGridSpec(
            num_scalar_prefetch=4,
            grid=(G, nj1),
            in_specs=[
                pl.BlockSpec((None, 1, R), lambda g, j, ge, gn, gm, gr: (gm[g], 0, 0), memory_space=pltpu.SMEM),
                pl.BlockSpec((None, 1, R), lambda g, j, ge, gn, gm, gr: (gm[jnp.minimum(g + 1, G - 1)], 0, 0),
                             memory_space=pltpu.SMEM),
                pl.BlockSpec(memory_space=pl.ANY),
                pl.BlockSpec((None, None, D, bf),
                             lambda g, j, ge, gn, gm, gr: (layer, ge[g], 0, jsel(g, j, gn, nj1 - 1))),
                pl.BlockSpec((None, None, D, bf),
                             lambda g, j, ge, gn, gm, gr: (layer, ge[g], 0, nj1 + jsel(g, j, gn, nj1 - 1))),
                pl.BlockSpec((None, None, 1, bf),
                             lambda g, j, ge, gn, gm, gr: (layer, ge[g], 0, jsel(g, j, gn, nj1 - 1))),
                pl.BlockSpec((None, None, 1, bf),
                             lambda g, j, ge, gn, gm, gr: (layer, ge[g], 0, nj1 + jsel(g, j, gn, nj1 - 1))),
            ],
            out_specs=pl.BlockSpec((R, bf), lambda g, j, ge, gn, gm, gr: (gm[g], jsel(g, j, gn, nj1 - 1))),
            scratch_shapes=[
                pltpu.VMEM((2, R, D // 2), jnp.uint32),
                pltpu.VMEM((R, D), BF16),
                pltpu.VMEM((D, 2 * bf), BF16),
                pltpu.SemaphoreType.DMA((2,)),
            ],
        ),
        out_shape=jax.ShapeDtypeStruct((GR, F), BF16),
        compiler_params=_cparams(2, vmem=MOE_UP_VMEM_LIMIT),
        name="moe_up",
    )(ge, gn, gm, gr, src3, src3, xp, w_gate_up, w_gate_up, b_gate_up.reshape(-1, E, 1, 2 * F),
      b_gate_up.reshape(-1, E, 1, 2 * F))

    return pl.pallas_call(
        _moe_down_kernel,
        grid_spec=pltpu.PrefetchScalarGridSpec(
            num_scalar_prefetch=4,
            grid=(G, nj2),
            in_specs=[
                pl.BlockSpec((R, F), lambda g, j, ge, gn, gm, gr: (gm[g], 0)),
                pl.BlockSpec((None, None, F, bn),
                             lambda g, j, ge, gn, gm, gr: (layer, ge[g], 0, jsel(g, j, gn, nj2 - 1))),
                pl.BlockSpec((None, None, 1, bn),
                             lambda g, j, ge, gn, gm, gr: (layer, ge[g], 0, jsel(g, j, gn, nj2 - 1))),
            ],
            out_specs=pl.BlockSpec((R, bn), lambda g, j, ge, gn, gm, gr: (gm[g], jsel(g, j, gn, nj2 - 1))),
            scratch_shapes=[pltpu.VMEM((F, bn), BF16)],
        ),
        out_shape=jax.ShapeDtypeStruct((GR, D), F32),
        compiler_params=_cparams(2),
        name="moe_down",
    )(ge, gn, gm, gr, act, w_down, b_down.reshape(-1, E, 1, D))


def _combine_ln_kernel(pos_ref, y_hbm, x_ref, tg_ref, g_ref, b_ref, o_ref, buf, sem, *, tm):
    def issue(r, c):
        for k in range(TOP_K):
            _row_copy(y_hbm, buf.at[k], sem, pos_ref[0, r * TOP_K + k], r).start()
        return c

    lax.fori_loop(0, tm, issue, 0)

    def drain(r, c):
        for k in range(TOP_K):
            _row_copy(y_hbm, buf.at[k], sem, 0, r).wait()
        return c

    lax.fori_loop(0, tm, drain, 0)
    tg = tg_ref[...]
    ffn = tg[:, 0:1] * buf[0]
    for k in range(1, TOP_K):
        ffn = ffn + tg[:, k:k + 1] * buf[k]
    o_ref[...] = _layer_norm(DEEPNORM_ALPHA * x_ref[...] + ffn, g_ref[...], b_ref[...])


def _combine_ln(ys, pos3, x, tg, ln_g, ln_b, layer, *, tm=COMBINE_TOKENS):
    M, D = x.shape
    row = pl.BlockSpec((tm, D), lambda i: (i, 0))
    vec = pl.BlockSpec((None, 1, D), lambda i: (layer, 0, 0))
    return pl.pallas_call(
        functools.partial(_combine_ln_kernel, tm=tm),
        grid=(M // tm,),
        in_specs=[
            pl.BlockSpec((None, 1, tm * TOP_K), lambda i: (i, 0, 0), memory_space=pltpu.SMEM),
            pl.BlockSpec(memory_space=pl.ANY),
            row,
            pl.BlockSpec((tm, LANES), lambda i: (i, 0)),
            vec, vec,
        ],
        out_specs=row,
        out_shape=jax.ShapeDtypeStruct((M, D), F32),
        scratch_shapes=[pltpu.VMEM((TOP_K, tm, D), F32), pltpu.SemaphoreType.DMA(())],
        compiler_params=_cparams(1),
        name="moe_combine_ln",
    )(pos3, ys, x, tg, ln_g.reshape(-1, 1, D), ln_b.reshape(-1, 1, D))


def _routing_tables(top_i, n_groups):
    R, sub = MOE_GROUP_ROWS, MOE_SUB_ROWS
    M = top_i.shape[0]
    e_flat = top_i.reshape(-1)
    onehot = (e_flat[:, None] == jnp.arange(N_EXPERTS, dtype=jnp.int32)[None, :]).astype(jnp.int32)
    csum = jnp.cumsum(onehot, axis=0)
    rank = jnp.take_along_axis(csum, e_flat[:, None], axis=1)[:, 0] - 1
    count = csum[-1]
    groups_e = (count + R - 1) // R
    gend = jnp.cumsum(groups_e)
    gbase = gend - groups_e
    n_used = gend[-1]
    pos = (gbase[e_flat] + rank // R) * R + rank % R
    gid = jnp.arange(n_groups, dtype=jnp.int32)
    gm = jnp.minimum(gid, n_used - 1)
    ge = jnp.searchsorted(gend, gm, side="right").astype(jnp.int32)
    used = gid < n_used
    gr = jnp.where(used, jnp.clip(count[ge] - (gm - gbase[ge]) * R, 0, R), 0).astype(jnp.int32)
    gn = (gr + sub - 1) // sub
    tok = jnp.arange(M * TOP_K, dtype=jnp.int32) // TOP_K
    src = jnp.zeros((n_groups * R,), jnp.int32).at[pos].set(tok)
    return pos.astype(jnp.int32), src, ge, gn, gm.astype(jnp.int32), gr


def _moe_layer(x1, xp, top_i, tg, w_gate_up, b_gate_up, w_down, b_down, ln_g, ln_b, layer):
    M, D = x1.shape
    R = MOE_GROUP_ROWS
    n_groups = N_EXPERTS + (M * TOP_K) // R
    pos, src, ge, gn, gm, gr = _routing_tables(top_i, n_groups)
    ys = _moe_experts(xp, src.reshape(n_groups, 1, R), ge, gn, gm, gr, w_gate_up, b_gate_up, w_down, b_down, layer)
    return _combine_ln(ys, pos.reshape(-1, 1, COMBINE_TOKENS * TOP_K), x1, tg, ln_g, ln_b, layer)


def kernel(x_prompt, x_sample, mem_prompt, cache_mem_k, cache_mem_v, state_hgrn, state_conv, hgrn_lb_logits, w_in_a, hgrn_gnorm, w_in_b, conv_w, w_mem_kv, w_out, ln1_g, ln1_b, router_w, router_b, w_gate_up, b_gate_up, w_down, b_down, ln2_g, ln2_b):
    Bp, Tp, D = x_prompt.shape
    Bs, Ts, _ = x_sample.shape
    n_mem = mem_prompt.shape[1]
    MEM = MEM_HEADS * MEM_HEAD_DIM
    TOK = HGRN_HEADS * HGRN_DV
    Mp, Ms = Bp * Tp, Bs * Ts
    M = Mp + Ms
    assert w_in_a.shape[-1] == 4 * TOK + MEM and w_in_b.shape[-1] == 3 * TOK + MEM
    assert w_gate_up.shape[1] == N_EXPERTS and w_out.shape[0] == DEPTH == 2

    memb = mem_prompt.reshape(Bp * n_mem, D).astype(BF16)
    kvs = [_matmul(memb, w_mem_kv, l, bm=Bp * n_mem, bn=512, out_dtype=F32, name="mem_kv") for l in range(DEPTH)]
    kv = jnp.stack(kvs).reshape(DEPTH, Bp, n_mem, 2 * MEM)
    new_mem_k = kv[..., :MEM].reshape(DEPTH, Bp, n_mem, MEM_HEADS, MEM_HEAD_DIM)
    new_mem_v = kv[..., MEM:].reshape(DEPTH, Bp, n_mem, MEM_HEADS, MEM_HEAD_DIM)
    cache_k = cache_mem_k.reshape(DEPTH, Bs, n_mem, MEM)
    cache_v = cache_mem_v.reshape(DEPTH, Bs, n_mem, MEM)

    lb_all = jnp.cumsum(jax.nn.softmax(hgrn_lb_logits.astype(F32), axis=0), axis=0)

    x = jnp.concatenate([x_prompt.reshape(Mp, D), x_sample.reshape(Ms, D)], axis=0)
    bm = M // 8
    for l in range(DEPTH):
        xb = x.astype(BF16)
        if l % 2 == 0:
            u = _matmul(xb, w_in_a, l // 2, bm=bm, bn=512, out_dtype=F32, name="in_proj_a")
            u3 = u[Mp:].reshape(Bs, Ts, -1)
            lb = lb_all[l]
            tok_p, hgrn_p = _hgrn_prompt(u, lb, hgrn_gnorm[l // 2], batch=Bp, seq=Tp)
            tok_s, hgrn_s = _hgrn_sample(u3, lb, hgrn_gnorm[l // 2], state_hgrn[l // 2])
            q_col = 4 * TOK // MEM
        else:
            u = _matmul(xb, w_in_b, l // 2, bm=bm, bn=512, out_dtype=F32, name="in_proj_b")
            u3 = u[Mp:].reshape(Bs, Ts, -1)
            tok_p, conv_p = _conv_prompt(u, conv_w[l // 2], batch=Bp, seq=Tp, dim=TOK)
            tok_s, conv_s = _conv_sample(u3, conv_w[l // 2], state_conv[l // 2], dim=TOK)
            q_col = 3 * TOK // MEM
        mem_p = _attn_prompt(u, kvs[l], batch=Bp, seq=Tp, n_mem=n_mem, q_col=q_col)
        mem_s = _attn_sample(u3, cache_k, cache_v, l, q_col=q_col)
        tok = jnp.concatenate([tok_p, tok_s.reshape(Ms, TOK).astype(BF16)], axis=0)
        mem = jnp.concatenate([mem_p, mem_s.reshape(Ms, MEM).astype(BF16)], axis=0)
        mix = _out_proj(tok, mem, w_out, l, bm=bm, bn=512)
        x1, xp, ti, tg = _ln_router(x, mix, ln1_g, ln1_b, router_w, router_b, l)
        x = _moe_layer(x1, xp, ti[:, :TOP_K], tg, w_gate_up, b_gate_up, w_down, b_down, ln2_g, ln2_b, l)

    y_prompt = x[:Mp].reshape(Bp, Tp, D)
    y_sample = x[Mp:].reshape(Bs, Ts, D)
    return (y_prompt, y_sample, new_mem_k, new_mem_v, hgrn_p[None], conv_p[None], hgrn_s[None], conv_s[None])
```

```python
import functools

import jax
import jax.numpy as jnp
from jax import lax
from jax.experimental import pallas as pl
from jax.experimental.pallas import tpu as pltpu

F32 = jnp.float32
BF16 = jnp.bfloat16

DEPTH = 2
HGRN_HEADS = 24
HGRN_DK = 128
HGRN_DV = 128
HGRN_CHUNK = 32
MEM_HEADS = 4
MEM_HEAD_DIM = 256
N_EXPERTS = 32
TOP_K = 4
CONV_W = 3
SWIGLU_LIMIT = 7.0
SWIGLU_ALPHA = 1.702
LN_EPS = 1e-5
RMS_EPS = 1e-6
DEEPNORM_ALPHA = (2.0 * DEPTH) ** 0.25

SUBLANES = 8
LANES = 128
VMEM_LIMIT = 56 * 1024 * 1024

MOE_GROUP_ROWS = 1280
MOE_SUB_ROWS = 128
COMBINE_TOKENS = 64


def _cparams(n_axes, vmem=VMEM_LIMIT):
    return pltpu.CompilerParams(dimension_semantics=("arbitrary",) * n_axes, vmem_limit_bytes=vmem)


def _sigmoid(x):
    return 1.0 / (1.0 + jnp.exp(-x))


def _mm_kernel(x_ref, w_ref, o_ref, wb_ref):
    @pl.when(pl.program_id(1) == 0)
    def _():
        wb_ref[...] = w_ref[...].astype(BF16)

    o_ref[...] = jnp.dot(x_ref[...], wb_ref[...], preferred_element_type=F32).astype(o_ref.dtype)


def _matmul(x, w3, layer, *, bm, bn, out_dtype, name):
    M, K = x.shape
    N = w3.shape[-1]
    assert M % bm == 0 and N % bn == 0
    return pl.pallas_call(
        _mm_kernel,
        grid=(N // bn, M // bm),
        in_specs=[
            pl.BlockSpec((bm, K), lambda j, i: (i, 0)),
            pl.BlockSpec((None, K, bn), lambda j, i: (layer, 0, j)),
        ],
        out_specs=pl.BlockSpec((bm, bn), lambda j, i: (i, j)),
        out_shape=jax.ShapeDtypeStruct((M, N), out_dtype),
        scratch_shapes=[pltpu.VMEM((K, bn), BF16)],
        compiler_params=_cparams(2),
        name=name,
    )(x, w3)


def _mm2_kernel(a_ref, b_ref, wa_ref, wb_ref, o_ref, wsa_ref, wsb_ref):
    @pl.when(pl.program_id(1) == 0)
    def _():
        wsa_ref[...] = wa_ref[...].astype(BF16)
        wsb_ref[...] = wb_ref[...].astype(BF16)

    acc = jnp.dot(a_ref[...], wsa_ref[...], preferred_element_type=F32)
    acc = acc + jnp.dot(b_ref[...], wsb_ref[...], preferred_element_type=F32)
    o_ref[...] = acc


def _out_proj(tok, mem, w_out, layer, *, bm, bn):
    M, Ka = tok.shape
    Kb = mem.shape[1]
    N = w_out.shape[-1]
    assert Ka % Kb == 0
    return pl.pallas_call(
        _mm2_kernel,
        grid=(N // bn, M // bm),
        in_specs=[
            pl.BlockSpec((bm, Ka), lambda j, i: (i, 0)),
            pl.BlockSpec((bm, Kb), lambda j, i: (i, 0)),
            pl.BlockSpec((None, Ka, bn), lambda j, i: (layer, 0, j)),
            pl.BlockSpec((None, Kb, bn), lambda j, i: (layer, Ka // Kb, j)),
        ],
        out_specs=pl.BlockSpec((bm, bn), lambda j, i: (i, j)),
        out_shape=jax.ShapeDtypeStruct((M, N), F32),
        scratch_shapes=[pltpu.VMEM((Ka, bn), BF16), pltpu.VMEM((Kb, bn), BF16)],
        compiler_params=_cparams(2),
        name="out_proj",
    )(tok, mem, w_out, w_out)


def _hgrn_chunk(uq, uf, ui, uo, lb, gn, st_in, st_out, kpad, gpad, vpad, *, rows, n_valid, state_kv):
    q = uq * _sigmoid(uq)
    forget = lb + (1.0 - lb) * _sigmoid(uf)
    kk = 1.0 - forget
    g = jnp.log(forget)
    v = ui
    if n_valid < rows:
        row = lax.broadcasted_iota(jnp.int32, (rows, LANES), 0)
        valid = row < n_valid
        kk = jnp.where(valid, kk, 0.0)
        g = jnp.where(valid, g, 0.0)
    G = g
    s = 1
    while s < n_valid:
        gpad[rows:, :] = G
        G = G + gpad[rows - s:2 * rows - s, :]
        s *= 2
    gpad[rows:, :] = G
    kpad[rows:, :] = kk
    vpad[rows:, :] = v
    parts = []
    for j in range(rows // SUBLANES):
        lo = j * SUBLANES
        if lo >= n_valid:
            parts.append(jnp.zeros((SUBLANES, LANES), F32))
            continue
        qj = q[lo:lo + SUBLANES, :]
        Gj = G[lo:lo + SUBLANES, :]
        acc = jnp.zeros((SUBLANES, LANES), F32)
        for d in range(min(lo + SUBLANES, n_valid)):
            a0 = rows + lo - d
            kd = kpad[a0:a0 + SUBLANES, :]
            Gd = gpad[a0:a0 + SUBLANES, :]
            vd = vpad[a0:a0 + SUBLANES, :]
            p = qj * kd * jnp.exp(Gj - Gd)
            acc = acc + jnp.sum(p, axis=-1, keepdims=True) * vd
        parts.append(acc)
    o = parts[0] if len(parts) == 1 else jnp.concatenate(parts, axis=0)
    st = st_in[...]
    qt = (q * jnp.exp(G)).astype(BF16)
    GL = G[n_valid - 1:n_valid, :]
    kt = (kk * jnp.exp(GL - G)).astype(BF16)
    contract0 = (((0,), (0,)), ((), ()))
    if state_kv:
        o = o + jnp.dot(qt, st.astype(BF16), preferred_element_type=F32)
        upd = lax.dot_general(kt, v.astype(BF16), contract0, preferred_element_type=F32)
        gl_rows = lax.dot_general(g, jnp.ones((rows, LANES), F32), contract0, preferred_element_type=F32,
                                  precision=lax.Precision.HIGHEST)
        st_out[...] = st * jnp.exp(gl_rows) + upd
    else:
        o = o + lax.dot_general(qt, st.astype(BF16), (((1,), (1,)), ((), ())), preferred_element_type=F32)
        upd = lax.dot_general(v.astype(BF16), kt, contract0, preferred_element_type=F32)
        st_out[...] = st * jnp.exp(GL) + upd
    ms = jnp.mean(o * o, axis=-1, keepdims=True)
    return o * lax.rsqrt(ms + RMS_EPS) * gn * (uo * _sigmoid(uo))


def _hgrn_prompt_kernel(uq_ref, uf_ref, ui_ref, uo_ref, lb_ref, gn_ref, tok_ref, sout_ref,
                        st_ref, kpad, gpad, vpad, *, tb, hb):
    c = pl.program_id(2)
    C = HGRN_CHUNK

    @pl.when(c == 0)
    def _():
        st_ref[...] = jnp.zeros_like(st_ref)
        kpad[...] = jnp.zeros_like(kpad)
        gpad[...] = jnp.zeros_like(gpad)
        vpad[...] = jnp.zeros_like(vpad)

    gn = gn_ref[...]

    def chunk(ci, carry):
        r0 = pl.multiple_of(ci * C, C)
        for i in range(hb):
            cols = slice(i * LANES, (i + 1) * LANES)
            out = _hgrn_chunk(uq_ref[pl.ds(r0, C), cols], uf_ref[pl.ds(r0, C), cols], ui_ref[pl.ds(r0, C), cols],
                              uo_ref[pl.ds(r0, C), cols], lb_ref[:, cols], gn, st_ref.at[i], st_ref.at[i],
                              kpad.at[i], gpad.at[i], vpad.at[i], rows=C, n_valid=C, state_kv=False)
            tok_ref[pl.ds(r0, C), cols] = out.astype(tok_ref.dtype)
        return carry

    lax.fori_loop(0, tb // C, chunk, 0)

    @pl.when(c == pl.num_programs(2) - 1)
    def _():
        for i in range(hb):
            sout_ref[i] = st_ref[i].T


def _hgrn_prompt(u, lb, gnorm, *, batch, seq, tb=256, hb=4):
    H = HGRN_HEADS
    nt = seq // tb
    nh = H // hb
    W = hb * LANES
    assert seq % tb == 0 and tb % HGRN_CHUNK == 0 and H % hb == 0
    col = lambda off: (lambda b, h, c: (b * nt + c, off + h))
    pad = pltpu.VMEM((hb, 2 * HGRN_CHUNK, LANES), F32)
    return pl.pallas_call(
        functools.partial(_hgrn_prompt_kernel, tb=tb, hb=hb),
        grid=(batch, nh, nt),
        in_specs=[
            pl.BlockSpec((tb, W), col(0)),
            pl.BlockSpec((tb, W), col(nh)),
            pl.BlockSpec((tb, W), col(2 * nh)),
            pl.BlockSpec((tb, W), col(3 * nh)),
            pl.BlockSpec((None, 1, W), lambda b, h, c: (h, 0, 0)),
            pl.BlockSpec((1, LANES), lambda b, h, c: (0, 0)),
        ],
        out_specs=[
            pl.BlockSpec((tb, W), lambda b, h, c: (b * nt + c, h)),
            pl.BlockSpec((None, hb, HGRN_DK, HGRN_DV), lambda b, h, c: (b, h, 0, 0)),
        ],
        out_shape=[
            jax.ShapeDtypeStruct((batch * seq, H * HGRN_DV), BF16),
            jax.ShapeDtypeStruct((batch, H, HGRN_DK, HGRN_DV), F32),
        ],
        scratch_shapes=[pltpu.VMEM((hb, HGRN_DV, HGRN_DK), F32), pad, pad, pad],
        compiler_params=_cparams(3),
        name="hgrn_prompt",
    )(u, u, u, u, lb.reshape(nh, 1, W), gnorm.reshape(1, LANES))


def _hgrn_sample_kernel(uq_ref, uf_ref, ui_ref, uo_ref, lb_ref, gn_ref, sin_ref, tok_ref, sout_ref,
                        kpad, gpad, vpad, inq, inf, ini, ino, *, bb, seq):
    R = SUBLANES

    @pl.when((pl.program_id(0) == 0) & (pl.program_id(1) == 0))
    def _():
        for r in (kpad, gpad, vpad, inq, inf, ini, ino):
            r[...] = jnp.zeros_like(r)

    lb = lb_ref[...]
    gn = gn_ref[...]
    for b in range(bb):
        inq[b, 0:seq, :] = uq_ref[b]
        inf[b, 0:seq, :] = uf_ref[b]
        ini[b, 0:seq, :] = ui_ref[b]
        ino[b, 0:seq, :] = uo_ref[b]
        out = _hgrn_chunk(inq[b], inf[b], ini[b], ino[b], lb, gn, sin_ref.at[b], sout_ref.at[b],
                          kpad.at[b], gpad.at[b], vpad.at[b], rows=R, n_valid=seq, state_kv=True)
        tok_ref[b] = out[0:seq, :]


def _hgrn_sample(u3, lb, gnorm, state, *, bb=8):
    B, T, _ = u3.shape
    H = HGRN_HEADS
    assert T <= SUBLANES and B % bb == 0
    col = lambda off: (lambda i, h: (i, 0, off + h))
    sspec = pl.BlockSpec((bb, None, HGRN_DK, HGRN_DV), lambda i, h: (i, h, 0, 0))
    pad = pltpu.VMEM((bb, 2 * SUBLANES, LANES), F32)
    row = pltpu.VMEM((bb, SUBLANES, LANES), F32)
    return pl.pallas_call(
        functools.partial(_hgrn_sample_kernel, bb=bb, seq=T),
        grid=(B // bb, H),
        in_specs=[
            pl.BlockSpec((bb, T, LANES), col(0)),
            pl.BlockSpec((bb, T, LANES), col(H)),
            pl.BlockSpec((bb, T, LANES), col(2 * H)),
            pl.BlockSpec((bb, T, LANES), col(3 * H)),
            pl.BlockSpec((None, 1, LANES), lambda i, h: (h, 0, 0)),
            pl.BlockSpec((1, LANES), lambda i, h: (0, 0)),
            sspec,
        ],
        out_specs=[pl.BlockSpec((bb, T, LANES), lambda i, h: (i, 0, h)), sspec],
        out_shape=[
            jax.ShapeDtypeStruct((B, T, H * HGRN_DV), F32),
            jax.ShapeDtypeStruct((B, H, HGRN_DK, HGRN_DV), F32),
        ],
        scratch_shapes=[pad, pad, pad, row, row, row, row],
        compiler_params=_cparams(2),
        name="hgrn_sample",
    )(u3, u3, u3, u3, lb.reshape(H, 1, LANES), gnorm.reshape(1, LANES), state)


def _conv_prompt_kernel(bg_ref, cg_ref, v_ref, w_ref, tok_ref, cout_ref, zpad, *, tb):
    t = pl.program_id(2)
    P = SUBLANES

    @pl.when(t == 0)
    def _():
        zpad[0:P, :] = jnp.zeros((P, zpad.shape[1]), F32)

    z = cg_ref[...] * v_ref[...]
    zpad[P:, :] = z
    w = w_ref[...]
    y = w[2:3, :] * z + w[1:2, :] * zpad[P - 1:P - 1 + tb, :] + w[0:1, :] * zpad[P - 2:P - 2 + tb, :]
    tok_ref[...] = (bg_ref[...] * y).astype(tok_ref.dtype)
    zpad[0:P, :] = zpad[tb:tb + P, :]

    @pl.when(t == pl.num_programs(2) - 1)
    def _():
        cout_ref[...] = zpad[P - (CONV_W - 1):P, :]


def _conv_prompt(u, conv_w, *, batch, seq, dim, tb=512, cb=512):
    nt = seq // tb
    nc = dim // cb
    assert seq % tb == 0 and dim % cb == 0
    col = lambda off: (lambda b, c, t: (b * nt + t, off + c))
    return pl.pallas_call(
        functools.partial(_conv_prompt_kernel, tb=tb),
        grid=(batch, nc, nt),
        in_specs=[
            pl.BlockSpec((tb, cb), col(0)),
            pl.BlockSpec((tb, cb), col(nc)),
            pl.BlockSpec((tb, cb), col(2 * nc)),
            pl.BlockSpec((CONV_W, cb), lambda b, c, t: (0, c)),
        ],
        out_specs=[
            pl.BlockSpec((tb, cb), lambda b, c, t: (b * nt + t, c)),
            pl.BlockSpec((None, CONV_W - 1, cb), lambda b, c, t: (b, 0, c)),
        ],
        out_shape=[
            jax.ShapeDtypeStruct((batch * seq, dim), BF16),
            jax.ShapeDtypeStruct((batch, CONV_W - 1, dim), F32),
        ],
        scratch_shapes=[pltpu.VMEM((tb + SUBLANES, cb), F32)],
        compiler_params=_cparams(3),
        name="conv_prompt",
    )(u, u, u, conv_w)


def _conv_sample_kernel(bg_ref, cg_ref, v_ref, w_ref, buf_ref, tok_ref, cout_ref, *, seq):
    w = w_ref[...]
    zp = [buf_ref[:, j, :] for j in range(CONV_W - 1)]
    zp += [cg_ref[:, t, :] * v_ref[:, t, :] for t in range(seq)]
    for t in range(seq):
        y = w[0:1, :] * zp[t]
        for j in range(1, CONV_W):
            y = y + w[j:j + 1, :] * zp[t + j]
        tok_ref[:, t, :] = bg_ref[:, t, :] * y
    for j in range(CONV_W - 1):
        cout_ref[:, j, :] = zp[seq + j]


def _conv_sample(u3, conv_w, buf, *, dim, bb=32, cb=512):
    B, T, _ = u3.shape
    nc = dim // cb
    assert B % bb == 0 and dim % cb == 0
    col = lambda off: (lambda i, c: (i, 0, off + c))
    bspec = pl.BlockSpec((bb, CONV_W - 1, cb), lambda i, c: (i, 0, c))
    return pl.pallas_call(
        functools.partial(_conv_sample_kernel, seq=T),
        grid=(B // bb, nc),
        in_specs=[
            pl.BlockSpec((bb, T, cb), col(0)),
            pl.BlockSpec((bb, T, cb), col(nc)),
            pl.BlockSpec((bb, T, cb), col(2 * nc)),
            pl.BlockSpec((CONV_W, cb), lambda i, c: (0, c)),
            bspec,
        ],
        out_specs=[pl.BlockSpec((bb, T, cb), lambda i, c: (i, 0, c)), bspec],
        out_shape=[
            jax.ShapeDtypeStruct((B, T, dim), F32),
            jax.ShapeDtypeStruct((B, CONV_W - 1, dim), F32),
        ],
        compiler_params=_cparams(2),
        name="conv_sample",
    )(u3, u3, u3, conv_w, buf)


def _attend(q, k, v):
    outs = []
    D = MEM_HEAD_DIM
    for h in range(MEM_HEADS):
        sl = slice(h * D, (h + 1) * D)
        s = lax.dot_general(q[:, sl].astype(BF16), k[:, sl].astype(BF16), (((1,), (1,)), ((), ())),
                            preferred_element_type=F32) * (D ** -0.5)
        m = jnp.max(s, axis=-1, keepdims=True)
        e = jnp.exp(s - m)
        p = e / jnp.sum(e, axis=-1, keepdims=True)
        outs.append(jnp.dot(p.astype(BF16), v[:, sl].astype(BF16), preferred_element_type=F32))
    return jnp.concatenate(outs, axis=-1)


def _attn_prompt_kernel(q_ref, k_ref, v_ref, o_ref):
    o_ref[...] = _attend(q_ref[...], k_ref[...], v_ref[...]).astype(o_ref.dtype)


def _attn_prompt(u, kv, *, batch, seq, n_mem, q_col, tq=512):
    W = MEM_HEADS * MEM_HEAD_DIM
    nt = seq // tq
    return pl.pallas_call(
        _attn_prompt_kernel,
        grid=(batch, nt),
        in_specs=[
            pl.BlockSpec((tq, W), lambda b, t: (b * nt + t, q_col)),
            pl.BlockSpec((n_mem, W), lambda b, t: (b, 0)),
            pl.BlockSpec((n_mem, W), lambda b, t: (b, 1)),
        ],
        out_specs=pl.BlockSpec((tq, W), lambda b, t: (b * nt + t, 0)),
        out_shape=jax.ShapeDtypeStruct((batch * seq, W), BF16),
        compiler_params=_cparams(2),
        name="attn_prompt",
    )(u, kv, kv)


def _attn_sample_kernel(q_ref, k_ref, v_ref, o_ref, qpad, *, bb, seq):
    @pl.when(pl.program_id(0) == 0)
    def _():
        qpad[...] = jnp.zeros_like(qpad)

    for b in range(bb):
        qpad[0:seq, :] = q_ref[b]
        o_ref[b] = _attend(qpad[...], k_ref[b], v_ref[b])[0:seq, :]


def _attn_sample(u3, mem_k, mem_v, layer, *, q_col, bb=4):
    B, T, _ = u3.shape
    n_mem, W = mem_k.shape[2], mem_k.shape[3]
    kspec = pl.BlockSpec((None, bb, n_mem, W), lambda i: (layer, i, 0, 0))
    return pl.pallas_call(
        functools.partial(_attn_sample_kernel, bb=bb, seq=T),
        grid=(B // bb,),
        in_specs=[pl.BlockSpec((bb, T, W), lambda i: (i, 0, q_col)), kspec, kspec],
        out_specs=pl.BlockSpec((bb, T, W), lambda i: (i, 0, 0)),
        out_shape=jax.ShapeDtypeStruct((B, T, W), F32),
        scratch_shapes=[pltpu.VMEM((SUBLANES, W), F32)],
        compiler_params=_cparams(1),
        name="attn_sample",
    )(u3, mem_k, mem_v)


def _layer_norm(h, g, b):
    mu = jnp.mean(h, axis=-1, keepdims=True)
    c = h - mu
    var = jnp.mean(c * c, axis=-1, keepdims=True)
    return c * lax.rsqrt(var + LN_EPS) * g + b


def _pack_bf16_pairs(y):
    w = y.shape[1] // 2
    bits = pltpu.bitcast(y.astype(BF16).astype(F32), jnp.uint32)
    return (bits[:, w:] & jnp.uint32(0xFFFF0000)) | (bits[:, :w] >> 16)


def _unpack_bf16_pairs(p):
    lo = pltpu.bitcast(p << 16, F32).astype(BF16)
    hi = pltpu.bitcast(p & jnp.uint32(0xFFFF0000), F32).astype(BF16)
    return lo, hi


def _ln_router_kernel(x_ref, mix_ref, g_ref, b_ref, rw_ref, rb_ref, y_ref, yp_ref, ti_ref, tg_ref):
    y = _layer_norm(DEEPNORM_ALPHA * x_ref[...] + mix_ref[...], g_ref[...], b_ref[...])
    y_ref[...] = y
    yp_ref[...] = _pack_bf16_pairs(y)
    logits = jnp.dot(y, rw_ref[...], preferred_element_type=F32, precision=lax.Precision.HIGHEST)
    logits = logits + rb_ref[...]
    n_e = logits.shape[-1]
    lane_e = lax.broadcasted_iota(jnp.int32, logits.shape, 1)
    lane = lax.broadcasted_iota(jnp.int32, ti_ref.shape, 1)
    vals = logits
    top_v, top_i = [], []
    for _ in range(TOP_K):
        m = jnp.max(vals, axis=-1, keepdims=True)
        idx = jnp.min(jnp.where(vals == m, lane_e, n_e), axis=-1, keepdims=True)
        top_v.append(m)
        top_i.append(idx)
        vals = jnp.where(lane_e == idx, -jnp.inf, vals)
    ex = [jnp.exp(tv - top_v[0]) for tv in top_v]
    den = ex[0]
    for e in ex[1:]:
        den = den + e
    ti = jnp.zeros(ti_ref.shape, jnp.int32)
    tg = jnp.zeros(tg_ref.shape, F32)
    for k in range(TOP_K):
        ti = jnp.where(lane == k, top_i[k], ti)
        tg = jnp.where(lane == k, ex[k] / den, tg)
    ti_ref[...] = ti
    tg_ref[...] = tg


def _ln_router(x, mix, ln_g, ln_b, router_w, router_b, layer, *, tm=256):
    M, D = x.shape
    E = router_w.shape[-1]
    row = pl.BlockSpec((tm, D), lambda i: (i, 0))
    half = pl.BlockSpec((tm, D // 2), lambda i: (i, 0))
    vec = pl.BlockSpec((None, 1, D), lambda i: (layer, 0, 0))
    wide = pl.BlockSpec((tm, LANES), lambda i: (i, 0))
    return pl.pallas_call(
        _ln_router_kernel,
        grid=(M // tm,),
        in_specs=[row, row, vec, vec,
                  pl.BlockSpec((None, D, E), lambda i: (layer, 0, 0)),
                  pl.BlockSpec((None, 1, E), lambda i: (layer, 0, 0))],
        out_specs=[row, half, wide, wide],
        out_shape=[jax.ShapeDtypeStruct((M, D), F32),
                   jax.ShapeDtypeStruct((M, D // 2), jnp.uint32),
                   jax.ShapeDtypeStruct((M, LANES), jnp.int32),
                   jax.ShapeDtypeStruct((M, LANES), F32)],
        compiler_params=_cparams(1),
        name="ln_router",
    )(x, mix, ln_g.reshape(-1, 1, D), ln_b.reshape(-1, 1, D), router_w, router_b.reshape(-1, 1, E))


def _row_copy(src_hbm, dst, sem, src_row, dst_row):
    return pltpu.make_async_copy(src_hbm.at[pl.ds(src_row, 1)], dst.at[pl.ds(dst_row, 1)], sem)


def _row_tiles(n_sub, tile_fn):
    sub = MOE_SUB_ROWS
    assert MOE_GROUP_ROWS // sub < 16
    for size in (8, 4, 2, 1):
        @pl.when((n_sub & size) != 0)
        def _(size=size):
            start = (n_sub & (-2 * size)) * sub
            tile_fn(pl.multiple_of(start, size * sub), size * sub)


def _zero_unused_rows(ref, n_sub, total_sub):
    sub = MOE_SUB_ROWS

    def fill(i, c):
        ref[pl.ds(pl.multiple_of(i * sub, sub), sub), :] = jnp.zeros((sub, ref.shape[1]), ref.dtype)
        return c

    lax.fori_loop(n_sub, total_sub, fill, 0)


def _moe_up_kernel(ge_ref, gn_ref, gm_ref, gr_ref, src_ref, srcn_ref, xp_hbm, wg_ref, wu_ref, bg_ref, bu_ref,
                   act_ref, xraw, xb, w_scr, sem, *, bf, chunk):
    g = pl.program_id(0)
    j = pl.program_id(1)
    n = gn_ref[g]
    sub = MOE_SUB_ROWS
    R, half = xraw.shape

    def row_fetch(idx_ref, r):
        return pltpu.make_async_copy(xp_hbm.at[pl.ds(idx_ref[0, r], 1)], xraw.at[pl.ds(r, 1)], sem)

    def drain_all():
        def drain(r, c):
            pltpu.make_async_copy(xp_hbm.at[pl.ds(0, 1)], xraw.at[pl.ds(0, 1)], sem).wait()
            return c

        lax.fori_loop(0, R, drain, 0)

    @pl.when((g == 0) & (j == 0))
    def _():
        def issue(r, c):
            row_fetch(src_ref, r).start()
            return c

        lax.fori_loop(0, R, issue, 0)

    @pl.when((n > 0) & (j == 0))
    def _():
        rows = gr_ref[g]
        drain_all()

        def unpack(t, c):
            r0 = pl.multiple_of(t * sub, sub)
            p = xraw[pl.ds(r0, sub), :]
            row = lax.broadcasted_iota(jnp.int32, p.shape, 0) + r0
            lo, hi = _unpack_bf16_pairs(jnp.where(row < rows, p, jnp.uint32(0)))
            xb[pl.ds(r0, sub), 0:half] = lo
            xb[pl.ds(r0, sub), half:2 * half] = hi
            return c

        lax.fori_loop(0, n, unpack, 0)

    @pl.when(n > 0)
    def _():
        for r in range(chunk):
            row_fetch(srcn_ref, j * chunk + r).start()
        w_scr[:, 0:bf] = wg_ref[...].astype(BF16)
        w_scr[:, bf:2 * bf] = wu_ref[...].astype(BF16)
        bg = bg_ref[...]
        bu = bu_ref[...]

        def tile(r0, rows):
            h = jnp.dot(xb[pl.ds(r0, rows), :], w_scr[...], preferred_element_type=F32)
            gate = jnp.minimum(h[:, 0:bf] + bg, SWIGLU_LIMIT)
            up = jnp.clip(h[:, bf:2 * bf] + bu, -SWIGLU_LIMIT, SWIGLU_LIMIT)
            act = (up + 1.0) * gate * _sigmoid(SWIGLU_ALPHA * gate)
            act_ref[pl.ds(r0, rows), :] = act.astype(act_ref.dtype)

        _row_tiles(n, tile)
        _zero_unused_rows(act_ref, n, act_ref.shape[0] // sub)

    @pl.when((g == pl.num_programs(0) - 1) & (j == pl.num_programs(1) - 1))
    def _():
        drain_all()


def _moe_down_kernel(ge_ref, gn_ref, gm_ref, gr_ref, a_ref, w_ref, b_ref, y_ref, w_scr):
    g = pl.program_id(0)
    n = gn_ref[g]

    @pl.when(n > 0)
    def _():
        w_scr[...] = w_ref[...].astype(BF16)
        bias = b_ref[...]

        def tile(r0, rows):
            y_ref[pl.ds(r0, rows), :] = jnp.dot(a_ref[pl.ds(r0, rows), :], w_scr[...],
                                                preferred_element_type=F32) + bias

        _row_tiles(n, tile)
        _zero_unused_rows(y_ref, n, y_ref.shape[0] // MOE_SUB_ROWS)


def _moe_experts(xp, src3, ge, gn, gm, gr, w_gate_up, b_gate_up, w_down, b_down, layer, *, bf=256, bn=512):
    R = MOE_GROUP_ROWS
    G = src3.shape[0]
    GR = G * R
    D = 2 * xp.shape[1]
    F = w_down.shape[2]
    E = w_down.shape[1]
    nj1, nj2 = F // bf, D // bn
    jsel = lambda g, j, gn_ref, last: jnp.where(gn_ref[g] > 0, j, last)
    assert R % nj1 == 0
    chunk = R // nj1

    act = pl.pallas_call(
        functools.partial(_moe_up_kernel, bf=bf, chunk=chunk),
        grid_spec=pltpu.PrefetchScalarGridSpec(
            num_scalar_prefetch=4,
            grid=(G, nj1),
            in_specs=[
                pl.BlockSpec((None, 1, R), lambda g, j, ge, gn, gm, gr: (gm[g], 0, 0), memory_space=pltpu.SMEM),
                pl.BlockSpec((None, 1, R), lambda g, j, ge, gn, gm, gr: (gm[jnp.minimum(g + 1, G - 1)], 0, 0),
                             memory_space=pltpu.SMEM),
                pl.BlockSpec(memory_space=pl.ANY),
                pl.BlockSpec((None, None, D, bf),
                             lambda g, j, ge, gn, gm, gr: (layer, ge[g], 0, jsel(g, j, gn, nj1 - 1))),
                pl.BlockSpec((None, None, D, bf),
                             lambda g, j, ge, gn, gm, gr: (layer, ge[g], 0, nj1 + jsel(g, j, gn, nj1 - 1))),
                pl.BlockSpec((None, None, 1, bf),
                             lambda g, j, ge, gn, gm, gr: (layer, ge[g], 0, jsel(g, j, gn, nj1 - 1))),
                pl.BlockSpec((None, None, 1, bf),
                             lambda g, j, ge, gn, gm, gr: (layer, ge[g], 0, nj1 + jsel(g, j, gn, nj1 - 1))),
            ],
            out_specs=pl.BlockSpec((R, bf), lambda g, j, ge, gn, gm, gr: (gm[g], jsel(g, j, gn, nj1 - 1))),
            scratch_shapes=[
                pltpu.VMEM((R, D // 2), jnp.uint32),
                pltpu.VMEM((R, D), BF16),
                pltpu.VMEM((D, 2 * bf), BF16),
                pltpu.SemaphoreType.DMA(()),
            ],
        ),
        out_shape=jax.ShapeDtypeStruct((GR, F), BF16),
        compiler_params=_cparams(2),
        name="moe_up",
    )(ge, gn, gm, gr, src3, src3, xp, w_gate_up, w_gate_up, b_gate_up.reshape(-1, E, 1, 2 * F),
      b_gate_up.reshape(-1, E, 1, 2 * F))

    return pl.pallas_call(
        _moe_down_kernel,
        grid_spec=pltpu.PrefetchScalarGridSpec(
            num_scalar_prefetch=4,
            grid=(G, nj2),
            in_specs=[
                pl.BlockSpec((R, F), lambda g, j, ge, gn, gm, gr: (gm[g], 0)),
                pl.BlockSpec((None, None, F, bn),
                             lambda g, j, ge, gn, gm, gr: (layer, ge[g], 0, jsel(g, j, gn, nj2 - 1))),
                pl.BlockSpec((None, None, 1, bn),
                             lambda g, j, ge, gn, gm, gr: (layer, ge[g], 0, jsel(g, j, gn, nj2 - 1))),
            ],
            out_specs=pl.BlockSpec((R, bn), lambda g, j, ge, gn, gm, gr: (gm[g], jsel(g, j, gn, nj2 - 1))),
            scratch_shapes=[pltpu.VMEM((F, bn), BF16)],
        ),
        out_shape=jax.ShapeDtypeStruct((GR, D), F32),
        compiler_params=_cparams(2),
        name="moe_down",
    )(ge, gn, gm, gr, act, w_down, b_down.reshape(-1, E, 1, D))


def _combine_ln_kernel(pos_ref, posn_ref, y_hbm, x_ref, tg_ref, g_ref, b_ref, o_ref, buf, sem, *, tm):
    i = pl.program_id(0)
    slot = i % 2

    def fetch(idx_ref, dst_slot, r, k):
        return _row_copy(y_hbm, buf.at[dst_slot, k], sem.at[dst_slot], idx_ref[0, r * TOP_K + k], r)

    def drain(dst_slot):
        def body(r, c):
            for k in range(TOP_K):
                _row_copy(y_hbm, buf.at[dst_slot, k], sem.at[dst_slot], 0, r).wait()
            return c

        lax.fori_loop(0, tm, body, 0)

    @pl.when(i == 0)
    def _():
        def issue(r, c):
            for k in range(TOP_K):
                fetch(pos_ref, 0, r, k).start()
            return c

        lax.fori_loop(0, tm, issue, 0)

    drain(slot)
    for r in range(tm):
        for k in range(TOP_K):
            fetch(posn_ref, 1 - slot, r, k).start()
    tg = tg_ref[...]
    ffn = tg[:, 0:1] * buf[slot, 0]
    for k in range(1, TOP_K):
        ffn = ffn + tg[:, k:k + 1] * buf[slot, k]
    o_ref[...] = _layer_norm(DEEPNORM_ALPHA * x_ref[...] + ffn, g_ref[...], b_ref[...])

    @pl.when(i == pl.num_programs(0) - 1)
    def _():
        drain(1 - slot)


def _combine_ln(ys, pos3, x, tg, ln_g, ln_b, layer, *, tm=COMBINE_TOKENS):
    M, D = x.shape
    n_tiles = M // tm
    row = pl.BlockSpec((tm, D), lambda i: (i, 0))
    vec = pl.BlockSpec((None, 1, D), lambda i: (layer, 0, 0))
    return pl.pallas_call(
        functools.partial(_combine_ln_kernel, tm=tm),
        grid=(n_tiles,),
        in_specs=[
            pl.BlockSpec((None, 1, tm * TOP_K), lambda i: (i, 0, 0), memory_space=pltpu.SMEM),
            pl.BlockSpec((None, 1, tm * TOP_K), lambda i: (jnp.minimum(i + 1, n_tiles - 1), 0, 0),
                         memory_space=pltpu.SMEM),
            pl.BlockSpec(memory_space=pl.ANY),
            row,
            pl.BlockSpec((tm, LANES), lambda i: (i, 0)),
            vec, vec,
        ],
        out_specs=row,
        out_shape=jax.ShapeDtypeStruct((M, D), F32),
        scratch_shapes=[pltpu.VMEM((2, TOP_K, tm, D), F32), pltpu.SemaphoreType.DMA((2,))],
        compiler_params=_cparams(1),
        name="moe_combine_ln",
    )(pos3, pos3, ys, x, tg, ln_g.reshape(-1, 1, D), ln_b.reshape(-1, 1, D))


def _routing_tables(top_i, n_groups):
    R, sub = MOE_GROUP_ROWS, MOE_SUB_ROWS
    M = top_i.shape[0]
    e_flat = top_i.reshape(-1)
    onehot = (e_flat[:, None] == jnp.arange(N_EXPERTS, dtype=jnp.int32)[None, :]).astype(jnp.int32)
    csum = jnp.cumsum(onehot, axis=0)
    rank = jnp.take_along_axis(csum, e_flat[:, None], axis=1)[:, 0] - 1
    count = csum[-1]
    groups_e = (count + R - 1) // R
    gend = jnp.cumsum(groups_e)
    gbase = gend - groups_e
    n_used = gend[-1]
    pos = (gbase[e_flat] + rank // R) * R + rank % R
    gid = jnp.arange(n_groups, dtype=jnp.int32)
    gm = jnp.minimum(gid, n_used - 1)
    ge = jnp.searchsorted(gend, gm, side="right").astype(jnp.int32)
    used = gid < n_used
    gr = jnp.where(used, jnp.clip(count[ge] - (gm - gbase[ge]) * R, 0, R), 0).astype(jnp.int32)
    gn = (gr + sub - 1) // sub
    tok = jnp.arange(M * TOP_K, dtype=jnp.int32) // TOP_K
    src = jnp.zeros((n_groups * R,), jnp.int32).at[pos].set(tok)
    return pos.astype(jnp.int32), src, ge, gn, gm.astype(jnp.int32), gr


def _moe_layer(x1, xp, top_i, tg, w_gate_up, b_gate_up, w_down, b_down, ln_g, ln_b, layer):
    M, D = x1.shape
    R = MOE_GROUP_ROWS
    n_groups = N_EXPERTS + (M * TOP_K) // R
    pos, src, ge, gn, gm, gr = _routing_tables(top_i, n_groups)
    ys = _moe_experts(xp, src.reshape(n_groups, 1, R), ge, gn, gm, gr, w_gate_up, b_gate_up, w_down, b_down, layer)
    return _combine_ln(ys, pos.reshape(-1, 1, COMBINE_TOKENS * TOP_K), x1, tg, ln_g, ln_b, layer)


def kernel(x_prompt, x_sample, mem_prompt, cache_mem_k, cache_mem_v, state_hgrn, state_conv, hgrn_lb_logits, w_in_a, hgrn_gnorm, w_in_b, conv_w, w_mem_kv, w_out, ln1_g, ln1_b, router_w, router_b, w_gate_up, b_gate_up, w_down, b_down, ln2_g, ln2_b):
    Bp, Tp, D = x_prompt.shape
    Bs, Ts, _ = x_sample.shape
    n_mem = mem_prompt.shape[1]
    MEM = MEM_HEADS * MEM_HEAD_DIM
    TOK = HGRN_HEADS * HGRN_DV
    Mp, Ms = Bp * Tp, Bs * Ts
    M = Mp + Ms
    assert w_in_a.shape[-1] == 4 * TOK + MEM and w_in_b.shape[-1] == 3 * TOK + MEM
    assert w_gate_up.shape[1] == N_EXPERTS and w_out.shape[0] == DEPTH == 2

    memb = mem_prompt.reshape(Bp * n_mem, D).astype(BF16)
    kvs = [_matmul(memb, w_mem_kv, l, bm=Bp * n_mem, bn=512, out_dtype=F32, name="mem_kv") for l in range(DEPTH)]
    kv = jnp.stack(kvs).reshape(DEPTH, Bp, n_mem, 2 * MEM)
    new_mem_k = kv[..., :MEM].reshape(DEPTH, Bp, n_mem, MEM_HEADS, MEM_HEAD_DIM)
    new_mem_v = kv[..., MEM:].reshape(DEPTH, Bp, n_mem, MEM_HEADS, MEM_HEAD_DIM)
    cache_k = cache_mem_k.reshape(DEPTH, Bs, n_mem, MEM)
    cache_v = cache_mem_v.reshape(DEPTH, Bs, n_mem, MEM)

    lb_all = jnp.cumsum(jax.nn.softmax(hgrn_lb_logits.astype(F32), axis=0), axis=0)

    x = jnp.concatenate([x_prompt.reshape(Mp, D), x_sample.reshape(Ms, D)], axis=0)
    bm = M // 8
    for l in range(DEPTH):
        xb = x.astype(BF16)
        if l % 2 == 0:
            u = _matmul(xb, w_in_a, l // 2, bm=bm, bn=512, out_dtype=F32, name="in_proj_a")
            u3 = u[Mp:].reshape(Bs, Ts, -1)
            lb = lb_all[l]
            tok_p, hgrn_p = _hgrn_prompt(u, lb, hgrn_gnorm[l // 2], batch=Bp, seq=Tp)
            tok_s, hgrn_s = _hgrn_sample(u3, lb, hgrn_gnorm[l // 2], state_hgrn[l // 2])
            q_col = 4 * TOK // MEM
        else:
            u = _matmul(xb, w_in_b, l // 2, bm=bm, bn=512, out_dtype=F32, name="in_proj_b")
            u3 = u[Mp:].reshape(Bs, Ts, -1)
            tok_p, conv_p = _conv_prompt(u, conv_w[l // 2], batch=Bp, seq=Tp, dim=TOK)
            tok_s, conv_s = _conv_sample(u3, conv_w[l // 2], state_conv[l // 2], dim=TOK)
            q_col = 3 * TOK // MEM
        mem_p = _attn_prompt(u, kvs[l], batch=Bp, seq=Tp, n_mem=n_mem, q_col=q_col)
        mem_s = _attn_sample(u3, cache_k, cache_v, l, q_col=q_col)
        tok = jnp.concatenate([tok_p, tok_s.reshape(Ms, TOK).astype(BF16)], axis=0)
        mem = jnp.concatenate([mem_p, mem_s.reshape(Ms, MEM).astype(BF16)], axis=0)
        mix = _out_proj(tok, mem, w_out, l, bm=bm, bn=512)
        x1, xp, ti, tg = _ln_router(x, mix, ln1_g, ln1_b, router_w, router_b, l)
        x = _moe_layer(x1, xp, ti[:, :TOP_K], tg, w_gate_up, b_gate_up, w_down, b_down, ln2_g, ln2_b, l)

    y_prompt = x[:Mp].reshape(Bp, Tp, D)
    y_sample = x[Mp:].reshape(Bs, Ts, D)
    return (y_prompt, y_sample, new_mem_k, new_mem_v, hgrn_p[None], conv_p[None], hgrn_s[None], conv_s[None])
```

```python
import functools

import jax
import jax.numpy as jnp
from jax import lax
from jax.experimental import pallas as pl
from jax.experimental.pallas import tpu as pltpu

F32 = jnp.float32
BF16 = jnp.bfloat16

DEPTH = 2
HGRN_HEADS = 24
HGRN_DK = 128
HGRN_DV = 128
HGRN_CHUNK = 32
MEM_HEADS = 4
MEM_HEAD_DIM = 256
N_EXPERTS = 32
TOP_K = 4
CONV_W = 3
SWIGLU_LIMIT = 7.0
SWIGLU_ALPHA = 1.702
LN_EPS = 1e-5
RMS_EPS = 1e-6
DEEPNORM_ALPHA = (2.0 * DEPTH) ** 0.25

SUBLANES = 8
LANES = 128
VMEM_LIMIT = 56 * 1024 * 1024

MOE_GROUP_ROWS = 1280
MOE_SUB_ROWS = 64
MOE_SPARE_GROUPS = 4
COMBINE_TOKENS = 64


def _cparams(n_axes, vmem=VMEM_LIMIT):
    return pltpu.CompilerParams(dimension_semantics=("arbitrary",) * n_axes, vmem_limit_bytes=vmem)


def _sigmoid(x):
    return 1.0 / (1.0 + jnp.exp(-x))


def _mm_kernel(x_ref, w_ref, o_ref, wb_ref):
    @pl.when(pl.program_id(1) == 0)
    def _():
        wb_ref[...] = w_ref[...].astype(BF16)

    o_ref[...] = jnp.dot(x_ref[...], wb_ref[...], preferred_element_type=F32).astype(o_ref.dtype)


def _matmul(x, w3, layer, *, bm, bn, out_dtype, name):
    M, K = x.shape
    N = w3.shape[-1]
    assert M % bm == 0 and N % bn == 0
    return pl.pallas_call(
        _mm_kernel,
        grid=(N // bn, M // bm),
        in_specs=[
            pl.BlockSpec((bm, K), lambda j, i: (i, 0)),
            pl.BlockSpec((None, K, bn), lambda j, i: (layer, 0, j)),
        ],
        out_specs=pl.BlockSpec((bm, bn), lambda j, i: (i, j)),
        out_shape=jax.ShapeDtypeStruct((M, N), out_dtype),
        scratch_shapes=[pltpu.VMEM((K, bn), BF16)],
        compiler_params=_cparams(2),
        name=name,
    )(x, w3)


def _mm2_kernel(a_ref, b_ref, wa_ref, wb_ref, o_ref, wsa_ref, wsb_ref):
    @pl.when(pl.program_id(1) == 0)
    def _():
        wsa_ref[...] = wa_ref[...].astype(BF16)
        wsb_ref[...] = wb_ref[...].astype(BF16)

    acc = jnp.dot(a_ref[...], wsa_ref[...], preferred_element_type=F32)
    acc = acc + jnp.dot(b_ref[...], wsb_ref[...], preferred_element_type=F32)
    o_ref[...] = acc


def _out_proj(tok, mem, w_out, layer, *, bm, bn):
    M, Ka = tok.shape
    Kb = mem.shape[1]
    N = w_out.shape[-1]
    assert Ka % Kb == 0
    return pl.pallas_call(
        _mm2_kernel,
        grid=(N // bn, M // bm),
        in_specs=[
            pl.BlockSpec((bm, Ka), lambda j, i: (i, 0)),
            pl.BlockSpec((bm, Kb), lambda j, i: (i, 0)),
            pl.BlockSpec((None, Ka, bn), lambda j, i: (layer, 0, j)),
            pl.BlockSpec((None, Kb, bn), lambda j, i: (layer, Ka // Kb, j)),
        ],
        out_specs=pl.BlockSpec((bm, bn), lambda j, i: (i, j)),
        out_shape=jax.ShapeDtypeStruct((M, N), F32),
        scratch_shapes=[pltpu.VMEM((Ka, bn), BF16), pltpu.VMEM((Kb, bn), BF16)],
        compiler_params=_cparams(2),
        name="out_proj",
    )(tok, mem, w_out, w_out)


def _hgrn_chunk(uq, uf, ui, uo, lb, gn, st_in, st_out, kpad, gpad, vpad, *, rows, n_valid, state_kv):
    q = uq * _sigmoid(uq)
    forget = lb + (1.0 - lb) * _sigmoid(uf)
    kk = 1.0 - forget
    g = jnp.log(forget)
    v = ui
    if n_valid < rows:
        row = lax.broadcasted_iota(jnp.int32, (rows, LANES), 0)
        valid = row < n_valid
        kk = jnp.where(valid, kk, 0.0)
        g = jnp.where(valid, g, 0.0)
    G = g
    s = 1
    while s < n_valid:
        gpad[rows:, :] = G
        G = G + gpad[rows - s:2 * rows - s, :]
        s *= 2
    gpad[rows:, :] = G
    kpad[rows:, :] = kk
    vpad[rows:, :] = v
    parts = []
    for j in range(rows // SUBLANES):
        lo = j * SUBLANES
        if lo >= n_valid:
            parts.append(jnp.zeros((SUBLANES, LANES), F32))
            continue
        qj = q[lo:lo + SUBLANES, :]
        Gj = G[lo:lo + SUBLANES, :]
        acc = jnp.zeros((SUBLANES, LANES), F32)
        for d in range(min(lo + SUBLANES, n_valid)):
            a0 = rows + lo - d
            kd = kpad[a0:a0 + SUBLANES, :]
            Gd = gpad[a0:a0 + SUBLANES, :]
            vd = vpad[a0:a0 + SUBLANES, :]
            p = qj * kd * jnp.exp(Gj - Gd)
            acc = acc + jnp.sum(p, axis=-1, keepdims=True) * vd
        parts.append(acc)
    o = parts[0] if len(parts) == 1 else jnp.concatenate(parts, axis=0)
    st = st_in[...]
    qt = (q * jnp.exp(G)).astype(BF16)
    GL = G[n_valid - 1:n_valid, :]
    kt = (kk * jnp.exp(GL - G)).astype(BF16)
    contract0 = (((0,), (0,)), ((), ()))
    if state_kv:
        o = o + jnp.dot(qt, st.astype(BF16), preferred_element_type=F32)
        upd = lax.dot_general(kt, v.astype(BF16), contract0, preferred_element_type=F32)
        gl_rows = lax.dot_general(g, jnp.ones((rows, LANES), F32), contract0, preferred_element_type=F32,
                                  precision=lax.Precision.HIGHEST)
        st_out[...] = st * jnp.exp(gl_rows) + upd
    else:
        o = o + lax.dot_general(qt, st.astype(BF16), (((1,), (1,)), ((), ())), preferred_element_type=F32)
        upd = lax.dot_general(v.astype(BF16), kt, contract0, preferred_element_type=F32)
        st_out[...] = st * jnp.exp(GL) + upd
    ms = jnp.mean(o * o, axis=-1, keepdims=True)
    return o * lax.rsqrt(ms + RMS_EPS) * gn * (uo * _sigmoid(uo))


def _hgrn_prompt_kernel(uq_ref, uf_ref, ui_ref, uo_ref, lb_ref, gn_ref, tok_ref, sout_ref,
                        st_ref, kpad, gpad, vpad, *, tb, hb):
    c = pl.program_id(2)
    C = HGRN_CHUNK

    @pl.when(c == 0)
    def _():
        st_ref[...] = jnp.zeros_like(st_ref)
        kpad[...] = jnp.zeros_like(kpad)
        gpad[...] = jnp.zeros_like(gpad)
        vpad[...] = jnp.zeros_like(vpad)

    gn = gn_ref[...]

    def chunk(ci, carry):
        r0 = pl.multiple_of(ci * C, C)
        for i in range(hb):
            cols = slice(i * LANES, (i + 1) * LANES)
            out = _hgrn_chunk(uq_ref[pl.ds(r0, C), cols], uf_ref[pl.ds(r0, C), cols], ui_ref[pl.ds(r0, C), cols],
                              uo_ref[pl.ds(r0, C), cols], lb_ref[:, cols], gn, st_ref.at[i], st_ref.at[i],
                              kpad.at[i], gpad.at[i], vpad.at[i], rows=C, n_valid=C, state_kv=False)
            tok_ref[pl.ds(r0, C), cols] = out.astype(tok_ref.dtype)
        return carry

    lax.fori_loop(0, tb // C, chunk, 0)

    @pl.when(c == pl.num_programs(2) - 1)
    def _():
        for i in range(hb):
            sout_ref[i] = st_ref[i].T


def _hgrn_prompt(u, lb, gnorm, *, batch, seq, tb=256, hb=4):
    H = HGRN_HEADS
    nt = seq // tb
    nh = H // hb
    W = hb * LANES
    assert seq % tb == 0 and tb % HGRN_CHUNK == 0 and H % hb == 0
    col = lambda off: (lambda b, h, c: (b * nt + c, off + h))
    pad = pltpu.VMEM((hb, 2 * HGRN_CHUNK, LANES), F32)
    return pl.pallas_call(
        functools.partial(_hgrn_prompt_kernel, tb=tb, hb=hb),
        grid=(batch, nh, nt),
        in_specs=[
            pl.BlockSpec((tb, W), col(0)),
            pl.BlockSpec((tb, W), col(nh)),
            pl.BlockSpec((tb, W), col(2 * nh)),
            pl.BlockSpec((tb, W), col(3 * nh)),
            pl.BlockSpec((None, 1, W), lambda b, h, c: (h, 0, 0)),
            pl.BlockSpec((1, LANES), lambda b, h, c: (0, 0)),
        ],
        out_specs=[
            pl.BlockSpec((tb, W), lambda b, h, c: (b * nt + c, h)),
            pl.BlockSpec((None, hb, HGRN_DK, HGRN_DV), lambda b, h, c: (b, h, 0, 0)),
        ],
        out_shape=[
            jax.ShapeDtypeStruct((batch * seq, H * HGRN_DV), BF16),
            jax.ShapeDtypeStruct((batch, H, HGRN_DK, HGRN_DV), F32),
        ],
        scratch_shapes=[pltpu.VMEM((hb, HGRN_DV, HGRN_DK), F32), pad, pad, pad],
        compiler_params=_cparams(3),
        name="hgrn_prompt",
    )(u, u, u, u, lb.reshape(nh, 1, W), gnorm.reshape(1, LANES))


def _hgrn_sample_kernel(uq_ref, uf_ref, ui_ref, uo_ref, lb_ref, gn_ref, sin_ref, tok_ref, sout_ref,
                        kpad, gpad, vpad, inq, inf, ini, ino, *, bb, seq):
    R = SUBLANES

    @pl.when((pl.program_id(0) == 0) & (pl.program_id(1) == 0))
    def _():
        for r in (kpad, gpad, vpad, inq, inf, ini, ino):
            r[...] = jnp.zeros_like(r)

    lb = lb_ref[...]
    gn = gn_ref[...]
    for b in range(bb):
        inq[b, 0:seq, :] = uq_ref[b]
        inf[b, 0:seq, :] = uf_ref[b]
        ini[b, 0:seq, :] = ui_ref[b]
        ino[b, 0:seq, :] = uo_ref[b]
        out = _hgrn_chunk(inq[b], inf[b], ini[b], ino[b], lb, gn, sin_ref.at[b], sout_ref.at[b],
                          kpad.at[b], gpad.at[b], vpad.at[b], rows=R, n_valid=seq, state_kv=True)
        tok_ref[b] = out[0:seq, :]


def _hgrn_sample(u3, lb, gnorm, state, *, bb=8):
    B, T, _ = u3.shape
    H = HGRN_HEADS
    assert T <= SUBLANES and B % bb == 0
    col = lambda off: (lambda i, h: (i, 0, off + h))
    sspec = pl.BlockSpec((bb, None, HGRN_DK, HGRN_DV), lambda i, h: (i, h, 0, 0))
    pad = pltpu.VMEM((bb, 2 * SUBLANES, LANES), F32)
    row = pltpu.VMEM((bb, SUBLANES, LANES), F32)
    return pl.pallas_call(
        functools.partial(_hgrn_sample_kernel, bb=bb, seq=T),
        grid=(B // bb, H),
        in_specs=[
            pl.BlockSpec((bb, T, LANES), col(0)),
            pl.BlockSpec((bb, T, LANES), col(H)),
            pl.BlockSpec((bb, T, LANES), col(2 * H)),
            pl.BlockSpec((bb, T, LANES), col(3 * H)),
            pl.BlockSpec((None, 1, LANES), lambda i, h: (h, 0, 0)),
            pl.BlockSpec((1, LANES), lambda i, h: (0, 0)),
            sspec,
        ],
        out_specs=[pl.BlockSpec((bb, T, LANES), lambda i, h: (i, 0, h)), sspec],
        out_shape=[
            jax.ShapeDtypeStruct((B, T, H * HGRN_DV), F32),
            jax.ShapeDtypeStruct((B, H, HGRN_DK, HGRN_DV), F32),
        ],
        scratch_shapes=[pad, pad, pad, row, row, row, row],
        compiler_params=_cparams(2),
        name="hgrn_sample",
    )(u3, u3, u3, u3, lb.reshape(H, 1, LANES), gnorm.reshape(1, LANES), state)


def _conv_prompt_kernel(bg_ref, cg_ref, v_ref, w_ref, tok_ref, cout_ref, zpad, *, tb):
    t = pl.program_id(2)
    P = SUBLANES

    @pl.when(t == 0)
    def _():
        zpad[0:P, :] = jnp.zeros((P, zpad.shape[1]), F32)

    z = cg_ref[...] * v_ref[...]
    zpad[P:, :] = z
    w = w_ref[...]
    y = w[2:3, :] * z + w[1:2, :] * zpad[P - 1:P - 1 + tb, :] + w[0:1, :] * zpad[P - 2:P - 2 + tb, :]
    tok_ref[...] = (bg_ref[...] * y).astype(tok_ref.dtype)
    zpad[0:P, :] = zpad[tb:tb + P, :]

    @pl.when(t == pl.num_programs(2) - 1)
    def _():
        cout_ref[...] = zpad[P - (CONV_W - 1):P, :]


def _conv_prompt(u, conv_w, *, batch, seq, dim, tb=512, cb=512):
    nt = seq // tb
    nc = dim // cb
    assert seq % tb == 0 and dim % cb == 0
    col = lambda off: (lambda b, c, t: (b * nt + t, off + c))
    return pl.pallas_call(
        functools.partial(_conv_prompt_kernel, tb=tb),
        grid=(batch, nc, nt),
        in_specs=[
            pl.BlockSpec((tb, cb), col(0)),
            pl.BlockSpec((tb, cb), col(nc)),
            pl.BlockSpec((tb, cb), col(2 * nc)),
            pl.BlockSpec((CONV_W, cb), lambda b, c, t: (0, c)),
        ],
        out_specs=[
            pl.BlockSpec((tb, cb), lambda b, c, t: (b * nt + t, c)),
            pl.BlockSpec((None, CONV_W - 1, cb), lambda b, c, t: (b, 0, c)),
        ],
        out_shape=[
            jax.ShapeDtypeStruct((batch * seq, dim), BF16),
            jax.ShapeDtypeStruct((batch, CONV_W - 1, dim), F32),
        ],
        scratch_shapes=[pltpu.VMEM((tb + SUBLANES, cb), F32)],
        compiler_params=_cparams(3),
        name="conv_prompt",
    )(u, u, u, conv_w)


def _conv_sample_kernel(bg_ref, cg_ref, v_ref, w_ref, buf_ref, tok_ref, cout_ref, *, seq):
    w = w_ref[...]
    zp = [buf_ref[:, j, :] for j in range(CONV_W - 1)]
    zp += [cg_ref[:, t, :] * v_ref[:, t, :] for t in range(seq)]
    for t in range(seq):
        y = w[0:1, :] * zp[t]
        for j in range(1, CONV_W):
            y = y + w[j:j + 1, :] * zp[t + j]
        tok_ref[:, t, :] = bg_ref[:, t, :] * y
    for j in range(CONV_W - 1):
        cout_ref[:, j, :] = zp[seq + j]


def _conv_sample(u3, conv_w, buf, *, dim, bb=32, cb=512):
    B, T, _ = u3.shape
    nc = dim // cb
    assert B % bb == 0 and dim % cb == 0
    col = lambda off: (lambda i, c: (i, 0, off + c))
    bspec = pl.BlockSpec((bb, CONV_W - 1, cb), lambda i, c: (i, 0, c))
    return pl.pallas_call(
        functools.partial(_conv_sample_kernel, seq=T),
        grid=(B // bb, nc),
        in_specs=[
            pl.BlockSpec((bb, T, cb), col(0)),
            pl.BlockSpec((bb, T, cb), col(nc)),
            pl.BlockSpec((bb, T, cb), col(2 * nc)),
            pl.BlockSpec((CONV_W, cb), lambda i, c: (0, c)),
            bspec,
        ],
        out_specs=[pl.BlockSpec((bb, T, cb), lambda i, c: (i, 0, c)), bspec],
        out_shape=[
            jax.ShapeDtypeStruct((B, T, dim), F32),
            jax.ShapeDtypeStruct((B, CONV_W - 1, dim), F32),
        ],
        compiler_params=_cparams(2),
        name="conv_sample",
    )(u3, u3, u3, conv_w, buf)


def _attend(q, k, v):
    outs = []
    D = MEM_HEAD_DIM
    for h in range(MEM_HEADS):
        sl = slice(h * D, (h + 1) * D)
        s = lax.dot_general(q[:, sl].astype(BF16), k[:, sl].astype(BF16), (((1,), (1,)), ((), ())),
                            preferred_element_type=F32) * (D ** -0.5)
        m = jnp.max(s, axis=-1, keepdims=True)
        e = jnp.exp(s - m)
        p = e / jnp.sum(e, axis=-1, keepdims=True)
        outs.append(jnp.dot(p.astype(BF16), v[:, sl].astype(BF16), preferred_element_type=F32))
    return jnp.concatenate(outs, axis=-1)


def _attn_prompt_kernel(q_ref, k_ref, v_ref, o_ref):
    o_ref[...] = _attend(q_ref[...], k_ref[...], v_ref[...]).astype(o_ref.dtype)


def _attn_prompt(u, kv, *, batch, seq, n_mem, q_col, tq=512):
    W = MEM_HEADS * MEM_HEAD_DIM
    nt = seq // tq
    return pl.pallas_call(
        _attn_prompt_kernel,
        grid=(batch, nt),
        in_specs=[
            pl.BlockSpec((tq, W), lambda b, t: (b * nt + t, q_col)),
            pl.BlockSpec((n_mem, W), lambda b, t: (b, 0)),
            pl.BlockSpec((n_mem, W), lambda b, t: (b, 1)),
        ],
        out_specs=pl.BlockSpec((tq, W), lambda b, t: (b * nt + t, 0)),
        out_shape=jax.ShapeDtypeStruct((batch * seq, W), BF16),
        compiler_params=_cparams(2),
        name="attn_prompt",
    )(u, kv, kv)


def _attn_sample_kernel(q_ref, k_ref, v_ref, o_ref, qpad, *, bb, seq):
    @pl.when(pl.program_id(0) == 0)
    def _():
        qpad[...] = jnp.zeros_like(qpad)

    for b in range(bb):
        qpad[0:seq, :] = q_ref[b]
        o_ref[b] = _attend(qpad[...], k_ref[b], v_ref[b])[0:seq, :]


def _attn_sample(u3, mem_k, mem_v, layer, *, q_col, bb=4):
    B, T, _ = u3.shape
    n_mem, W = mem_k.shape[2], mem_k.shape[3]
    kspec = pl.BlockSpec((None, bb, n_mem, W), lambda i: (layer, i, 0, 0))
    return pl.pallas_call(
        functools.partial(_attn_sample_kernel, bb=bb, seq=T),
        grid=(B // bb,),
        in_specs=[pl.BlockSpec((bb, T, W), lambda i: (i, 0, q_col)), kspec, kspec],
        out_specs=pl.BlockSpec((bb, T, W), lambda i: (i, 0, 0)),
        out_shape=jax.ShapeDtypeStruct((B, T, W), F32),
        scratch_shapes=[pltpu.VMEM((SUBLANES, W), F32)],
        compiler_params=_cparams(1),
        name="attn_sample",
    )(u3, mem_k, mem_v)


def _layer_norm(h, g, b):
    mu = jnp.mean(h, axis=-1, keepdims=True)
    c = h - mu
    var = jnp.mean(c * c, axis=-1, keepdims=True)
    return c * lax.rsqrt(var + LN_EPS) * g + b


def _pack_bf16_pairs(y):
    w = y.shape[1] // 2
    bits = pltpu.bitcast(y.astype(BF16).astype(F32), jnp.uint32)
    return (bits[:, w:] & jnp.uint32(0xFFFF0000)) | (bits[:, :w] >> 16)


def _unpack_bf16_pairs(p):
    lo = pltpu.bitcast(p << 16, F32).astype(BF16)
    hi = pltpu.bitcast(p & jnp.uint32(0xFFFF0000), F32).astype(BF16)
    return lo, hi


def _ln_router_kernel(x_ref, mix_ref, g_ref, b_ref, rw_ref, rb_ref, y_ref, yp_ref, ti_ref, tg_ref):
    y = _layer_norm(DEEPNORM_ALPHA * x_ref[...] + mix_ref[...], g_ref[...], b_ref[...])
    y_ref[...] = y
    yp_ref[...] = _pack_bf16_pairs(y)
    logits = jnp.dot(y, rw_ref[...], preferred_element_type=F32, precision=lax.Precision.HIGHEST)
    logits = logits + rb_ref[...]
    n_e = logits.shape[-1]
    lane_e = lax.broadcasted_iota(jnp.int32, logits.shape, 1)
    lane = lax.broadcasted_iota(jnp.int32, ti_ref.shape, 1)
    vals = logits
    top_v, top_i = [], []
    for _ in range(TOP_K):
        m = jnp.max(vals, axis=-1, keepdims=True)
        idx = jnp.min(jnp.where(vals == m, lane_e, n_e), axis=-1, keepdims=True)
        top_v.append(m)
        top_i.append(idx)
        vals = jnp.where(lane_e == idx, -jnp.inf, vals)
    ex = [jnp.exp(tv - top_v[0]) for tv in top_v]
    den = ex[0]
    for e in ex[1:]:
        den = den + e
    ti = jnp.zeros(ti_ref.shape, jnp.int32)
    tg = jnp.zeros(tg_ref.shape, F32)
    for k in range(TOP_K):
        ti = jnp.where(lane == k, top_i[k], ti)
        tg = jnp.where(lane == k, ex[k] / den, tg)
    ti_ref[...] = ti
    tg_ref[...] = tg


def _ln_router(x, mix, ln_g, ln_b, router_w, router_b, layer, *, tm=256):
    M, D = x.shape
    E = router_w.shape[-1]
    row = pl.BlockSpec((tm, D), lambda i: (i, 0))
    half = pl.BlockSpec((tm, D // 2), lambda i: (i, 0))
    vec = pl.BlockSpec((None, 1, D), lambda i: (layer, 0, 0))
    wide = pl.BlockSpec((tm, LANES), lambda i: (i, 0))
    return pl.pallas_call(
        _ln_router_kernel,
        grid=(M // tm,),
        in_specs=[row, row, vec, vec,
                  pl.BlockSpec((None, D, E), lambda i: (layer, 0, 0)),
                  pl.BlockSpec((None, 1, E), lambda i: (layer, 0, 0))],
        out_specs=[row, half, wide, wide],
        out_shape=[jax.ShapeDtypeStruct((M, D), F32),
                   jax.ShapeDtypeStruct((M, D // 2), jnp.uint32),
                   jax.ShapeDtypeStruct((M, LANES), jnp.int32),
                   jax.ShapeDtypeStruct((M, LANES), F32)],
        compiler_params=_cparams(1),
        name="ln_router",
    )(x, mix, ln_g.reshape(-1, 1, D), ln_b.reshape(-1, 1, D), router_w, router_b.reshape(-1, 1, E))


def _row_copy(src_hbm, dst, sem, src_row, dst_row):
    return pltpu.make_async_copy(src_hbm.at[pl.ds(src_row, 1)], dst.at[pl.ds(dst_row, 1)], sem)


def _row_tiles(n_sub, tile_fn):
    sub = MOE_SUB_ROWS
    top = 1
    while 2 * top <= MOE_GROUP_ROWS // sub:
        top *= 2
    sizes = []
    while top >= 1:
        sizes.append(top)
        top //= 2
    for size in sizes:
        @pl.when((n_sub & size) != 0)
        def _(size=size):
            start = (n_sub & (-2 * size)) * sub
            tile_fn(pl.multiple_of(start, size * sub), size * sub)


def _zero_unused_rows(ref, n_sub, total_sub):
    sub = MOE_SUB_ROWS

    def fill(i, c):
        ref[pl.ds(pl.multiple_of(i * sub, sub), sub), :] = jnp.zeros((sub, ref.shape[1]), ref.dtype)
        return c

    lax.fori_loop(n_sub, total_sub, fill, 0)


def _moe_up_kernel(ge_ref, gn_ref, gm_ref, gr_ref, src_ref, srcn_ref, xp_hbm, wg_ref, wu_ref, bg_ref, bu_ref,
                   act_ref, xraw, xb, w_scr, sem, *, bf, chunk):
    g = pl.program_id(0)
    j = pl.program_id(1)
    n = gn_ref[g]
    sub = MOE_SUB_ROWS
    R, half = xraw.shape

    def row_fetch(idx_ref, r):
        return pltpu.make_async_copy(xp_hbm.at[pl.ds(idx_ref[0, r], 1)], xraw.at[pl.ds(r, 1)], sem)

    def drain_all():
        def drain(r, c):
            pltpu.make_async_copy(xp_hbm.at[pl.ds(0, 1)], xraw.at[pl.ds(0, 1)], sem).wait()
            return c

        lax.fori_loop(0, R, drain, 0)

    @pl.when((g == 0) & (j == 0))
    def _():
        def issue(r, c):
            row_fetch(src_ref, r).start()
            return c

        lax.fori_loop(0, R, issue, 0)

    @pl.when((n > 0) & (j == 0))
    def _():
        rows = gr_ref[g]
        drain_all()

        def unpack(t, c):
            r0 = pl.multiple_of(t * sub, sub)
            p = xraw[pl.ds(r0, sub), :]
            row = lax.broadcasted_iota(jnp.int32, p.shape, 0) + r0
            lo, hi = _unpack_bf16_pairs(jnp.where(row < rows, p, jnp.uint32(0)))
            xb[pl.ds(r0, sub), 0:half] = lo
            xb[pl.ds(r0, sub), half:2 * half] = hi
            return c

        lax.fori_loop(0, n, unpack, 0)

    @pl.when(n > 0)
    def _():
        for r in range(chunk):
            row_fetch(srcn_ref, j * chunk + r).start()
        w_scr[:, 0:bf] = wg_ref[...].astype(BF16)
        w_scr[:, bf:2 * bf] = wu_ref[...].astype(BF16)
        bg = bg_ref[...]
        bu = bu_ref[...]

        def tile(r0, rows):
            h = jnp.dot(xb[pl.ds(r0, rows), :], w_scr[...], preferred_element_type=F32)
            gate = jnp.minimum(h[:, 0:bf] + bg, SWIGLU_LIMIT)
            up = jnp.clip(h[:, bf:2 * bf] + bu, -SWIGLU_LIMIT, SWIGLU_LIMIT)
            act = (up + 1.0) * gate * _sigmoid(SWIGLU_ALPHA * gate)
            act_ref[pl.ds(r0, rows), :] = act.astype(act_ref.dtype)

        _row_tiles(n, tile)
        _zero_unused_rows(act_ref, n, act_ref.shape[0] // sub)

    @pl.when((g == pl.num_programs(0) - 1) & (j == pl.num_programs(1) - 1))
    def _():
        drain_all()


def _moe_down_kernel(ge_ref, gn_ref, gm_ref, gr_ref, a_ref, w_ref, b_ref, y_ref, w_scr):
    g = pl.program_id(0)
    n = gn_ref[g]

    @pl.when(n > 0)
    def _():
        w_scr[...] = w_ref[...].astype(BF16)
        bias = b_ref[...]

        def tile(r0, rows):
            y_ref[pl.ds(r0, rows), :] = jnp.dot(a_ref[pl.ds(r0, rows), :], w_scr[...],
                                                preferred_element_type=F32) + bias

        _row_tiles(n, tile)
        _zero_unused_rows(y_ref, n, y_ref.shape[0] // MOE_SUB_ROWS)


def _moe_experts(xp, src3, ge, gn, gm, gr, w_gate_up, b_gate_up, w_down, b_down, layer, *, bf=256, bn=512):
    R = MOE_GROUP_ROWS
    G = src3.shape[0]
    GR = G * R
    D = 2 * xp.shape[1]
    F = w_down.shape[2]
    E = w_down.shape[1]
    nj1, nj2 = F // bf, D // bn
    jsel = lambda g, j, gn_ref, last: jnp.where(gn_ref[g] > 0, j, last)
    assert R % nj1 == 0
    chunk = R // nj1

    act = pl.pallas_call(
        functools.partial(_moe_up_kernel, bf=bf, chunk=chunk),
        grid_spec=pltpu.PrefetchScalarGridSpec(
            num_scalar_prefetch=4,
            grid=(G, nj1),
            in_specs=[
                pl.BlockSpec((None, 1, R), lambda g, j, ge, gn, gm, gr: (gm[g], 0, 0), memory_space=pltpu.SMEM),
                pl.BlockSpec((None, 1, R), lambda g, j, ge, gn, gm, gr: (gm[jnp.minimum(g + 1, G - 1)], 0, 0),
                             memory_space=pltpu.SMEM),
                pl.BlockSpec(memory_space=pl.ANY),
                pl.BlockSpec((None, None, D, bf),
                             lambda g, j, ge, gn, gm, gr: (layer, ge[g], 0, jsel(g, j, gn, nj1 - 1))),
                pl.BlockSpec((None, None, D, bf),
                             lambda g, j, ge, gn, gm, gr: (layer, ge[g], 0, nj1 + jsel(g, j, gn, nj1 - 1))),
                pl.BlockSpec((None, None, 1, bf),
                             lambda g, j, ge, gn, gm, gr: (layer, ge[g], 0, jsel(g, j, gn, nj1 - 1))),
                pl.BlockSpec((None, None, 1, bf),
                             lambda g, j, ge, gn, gm, gr: (layer, ge[g], 0, nj1 + jsel(g, j, gn, nj1 - 1))),
            ],
            out_specs=pl.BlockSpec((R, bf), lambda g, j, ge, gn, gm, gr: (gm[g], jsel(g, j, gn, nj1 - 1))),
            scratch_shapes=[
                pltpu.VMEM((R, D // 2), jnp.uint32),
                pltpu.VMEM((R, D), BF16),
                pltpu.VMEM((D, 2 * bf), BF16),
                pltpu.SemaphoreType.DMA(()),
            ],
        ),
        out_shape=jax.ShapeDtypeStruct((GR, F), BF16),
        compiler_params=_cparams(2),
        name="moe_up",
    )(ge, gn, gm, gr, src3, src3, xp, w_gate_up, w_gate_up, b_gate_up.reshape(-1, E, 1, 2 * F),
      b_gate_up.reshape(-1, E, 1, 2 * F))

    return pl.pallas_call(
        _moe_down_kernel,
        grid_spec=pltpu.PrefetchScalarGridSpec(
            num_scalar_prefetch=4,
            grid=(G, nj2),
            in_specs=[
                pl.BlockSpec((R, F), lambda g, j, ge, gn, gm, gr: (gm[g], 0)),
                pl.BlockSpec((None, None, F, bn),
                             lambda g, j, ge, gn, gm, gr: (layer, ge[g], 0, jsel(g, j, gn, nj2 - 1))),
                pl.BlockSpec((None, None, 1, bn),
                             lambda g, j, ge, gn, gm, gr: (layer, ge[g], 0, jsel(g, j, gn, nj2 - 1))),
            ],
            out_specs=pl.BlockSpec((R, bn), lambda g, j, ge, gn, gm, gr: (gm[g], jsel(g, j, gn, nj2 - 1))),
            scratch_shapes=[pltpu.VMEM((F, bn), BF16)],
        ),
        out_shape=jax.ShapeDtypeStruct((GR, D), F32),
        compiler_params=_cparams(2),
        name="moe_down",
    )(ge, gn, gm, gr, act, w_down, b_down.reshape(-1, E, 1, D))


def _combine_ln_kernel(pos_ref, posn_ref, y_hbm, x_ref, tg_ref, g_ref, b_ref, o_ref, buf, sem, *, tm):
    i = pl.program_id(0)
    slot = i % 2

    def fetch(idx_ref, dst_slot, r, k):
        return _row_copy(y_hbm, buf.at[dst_slot, k], sem.at[dst_slot], idx_ref[0, r * TOP_K + k], r)

    def drain(dst_slot):
        def body(r, c):
            for k in range(TOP_K):
                _row_copy(y_hbm, buf.at[dst_slot, k], sem.at[dst_slot], 0, r).wait()
            return c

        lax.fori_loop(0, tm, body, 0)

    @pl.when(i == 0)
    def _():
        def issue(r, c):
            for k in range(TOP_K):
                fetch(pos_ref, 0, r, k).start()
            return c

        lax.fori_loop(0, tm, issue, 0)

    drain(slot)
    for r in range(tm):
        for k in range(TOP_K):
            fetch(posn_ref, 1 - slot, r, k).start()
    tg = tg_ref[...]
    ffn = tg[:, 0:1] * buf[slot, 0]
    for k in range(1, TOP_K):
        ffn = ffn + tg[:, k:k + 1] * buf[slot, k]
    o_ref[...] = _layer_norm(DEEPNORM_ALPHA * x_ref[...] + ffn, g_ref[...], b_ref[...])

    @pl.when(i == pl.num_programs(0) - 1)
    def _():
        drain(1 - slot)


def _combine_ln(ys, pos3, x, tg, ln_g, ln_b, layer, *, tm=COMBINE_TOKENS):
    M, D = x.shape
    n_tiles = M // tm
    row = pl.BlockSpec((tm, D), lambda i: (i, 0))
    vec = pl.BlockSpec((None, 1, D), lambda i: (layer, 0, 0))
    return pl.pallas_call(
        functools.partial(_combine_ln_kernel, tm=tm),
        grid=(n_tiles,),
        in_specs=[
            pl.BlockSpec((None, 1, tm * TOP_K), lambda i: (i, 0, 0), memory_space=pltpu.SMEM),
            pl.BlockSpec((None, 1, tm * TOP_K), lambda i: (jnp.minimum(i + 1, n_tiles - 1), 0, 0),
                         memory_space=pltpu.SMEM),
            pl.BlockSpec(memory_space=pl.ANY),
            row,
            pl.BlockSpec((tm, LANES), lambda i: (i, 0)),
            vec, vec,
        ],
        out_specs=row,
        out_shape=jax.ShapeDtypeStruct((M, D), F32),
        scratch_shapes=[pltpu.VMEM((2, TOP_K, tm, D), F32), pltpu.SemaphoreType.DMA((2,))],
        compiler_params=_cparams(1),
        name="moe_combine_ln",
    )(pos3, pos3, ys, x, tg, ln_g.reshape(-1, 1, D), ln_b.reshape(-1, 1, D))


def _routing_tables(top_i, n_groups):
    R, sub = MOE_GROUP_ROWS, MOE_SUB_ROWS
    M = top_i.shape[0]
    e_flat = top_i.reshape(-1)
    blk = LANES
    assert (M * TOP_K) % blk == 0
    onehot = (e_flat[:, None] == jnp.arange(N_EXPERTS, dtype=jnp.int32)[None, :]).astype(F32)
    onehot = onehot.reshape(-1, blk, N_EXPERTS)
    within = jnp.einsum("ts,bse->bte", jnp.tril(jnp.ones((blk, blk), F32)), onehot)
    totals = within[:, -1, :]
    before = jnp.cumsum(totals, axis=0) - totals
    rank = jnp.sum((within + before[:, None, :]) * onehot, axis=-1).reshape(-1).astype(jnp.int32) - 1
    count = (before[-1] + totals[-1]).astype(jnp.int32)
    groups_e = (count + R - 1) // R
    gend = jnp.cumsum(groups_e)
    gbase = gend - groups_e
    n_used = gend[-1]
    pos = (gbase[e_flat] + rank // R) * R + rank % R
    gid = jnp.arange(n_groups, dtype=jnp.int32)
    gm = jnp.minimum(gid, n_used - 1)
    ge = jnp.searchsorted(gend, gm, side="right").astype(jnp.int32)
    used = gid < n_used
    gr = jnp.where(used, jnp.clip(count[ge] - (gm - gbase[ge]) * R, 0, R), 0).astype(jnp.int32)
    gn = (gr + sub - 1) // sub
    tok = jnp.arange(M * TOP_K, dtype=jnp.int32) // TOP_K
    src = jnp.zeros((n_groups * R,), jnp.int32).at[pos].set(tok)
    return pos.astype(jnp.int32), src, ge, gn, gm.astype(jnp.int32), gr, n_used


def _moe_layer(x1, xp, top_i, tg, w_gate_up, b_gate_up, w_down, b_down, ln_g, ln_b, layer):
    M, D = x1.shape
    R = MOE_GROUP_ROWS
    n_full = N_EXPERTS + (M * TOP_K) // R
    n_common = min(n_full, N_EXPERTS + MOE_SPARE_GROUPS)
    pos, src, ge, gn, gm, gr, n_used = _routing_tables(top_i, n_full)
    pos3 = pos.reshape(-1, 1, COMBINE_TOKENS * TOP_K)

    def run(G):
        def f():
            ys = _moe_experts(xp, src[:G * R].reshape(G, 1, R), ge[:G], gn[:G], gm[:G], gr[:G],
                              w_gate_up, b_gate_up, w_down, b_down, layer)
            return _combine_ln(ys, pos3, x1, tg, ln_g, ln_b, layer)

        return f

    if n_common == n_full:
        return run(n_full)()
    return lax.cond(n_used <= n_common, run(n_common), run(n_full))


def kernel(x_prompt, x_sample, mem_prompt, cache_mem_k, cache_mem_v, state_hgrn, state_conv, hgrn_lb_logits, w_in_a, hgrn_gnorm, w_in_b, conv_w, w_mem_kv, w_out, ln1_g, ln1_b, router_w, router_b, w_gate_up, b_gate_up, w_down, b_down, ln2_g, ln2_b):
    Bp, Tp, D = x_prompt.shape
    Bs, Ts, _ = x_sample.shape
    n_mem = mem_prompt.shape[1]
    MEM = MEM_HEADS * MEM_HEAD_DIM
    TOK = HGRN_HEADS * HGRN_DV
    Mp, Ms = Bp * Tp, Bs * Ts
    M = Mp + Ms
    assert w_in_a.shape[-1] == 4 * TOK + MEM and w_in_b.shape[-1] == 3 * TOK + MEM
    assert w_gate_up.shape[1] == N_EXPERTS and w_out.shape[0] == DEPTH == 2

    memb = mem_prompt.reshape(Bp * n_mem, D).astype(BF16)
    kvs = [_matmul(memb, w_mem_kv, l, bm=Bp * n_mem, bn=512, out_dtype=F32, name="mem_kv") for l in range(DEPTH)]
    kv = jnp.stack(kvs).reshape(DEPTH, Bp, n_mem, 2 * MEM)
    new_mem_k = kv[..., :MEM].reshape(DEPTH, Bp, n_mem, MEM_HEADS, MEM_HEAD_DIM)
    new_mem_v = kv[..., MEM:].reshape(DEPTH, Bp, n_mem, MEM_HEADS, MEM_HEAD_DIM)
    cache_k = cache_mem_k.reshape(DEPTH, Bs, n_mem, MEM)
    cache_v = cache_mem_v.reshape(DEPTH, Bs, n_mem, MEM)

    lb_all = jnp.cumsum(jax.nn.softmax(hgrn_lb_logits.astype(F32), axis=0), axis=0)

    x = jnp.concatenate([x_prompt.reshape(Mp, D), x_sample.reshape(Ms, D)], axis=0)
    bm = M // 8
    for l in range(DEPTH):
        xb = x.astype(BF16)
        if l % 2 == 0:
            u = _matmul(xb, w_in_a, l // 2, bm=bm, bn=512, out_dtype=F32, name="in_proj_a")
            u3 = u[Mp:].reshape(Bs, Ts, -1)
            lb = lb_all[l]
            tok_p, hgrn_p = _hgrn_prompt(u, lb, hgrn_gnorm[l // 2], batch=Bp, seq=Tp)
            tok_s, hgrn_s = _hgrn_sample(u3, lb, hgrn_gnorm[l // 2], state_hgrn[l // 2])
            q_col = 4 * TOK // MEM
        else:
            u = _matmul(xb, w_in_b, l // 2, bm=bm, bn=512, out_dtype=F32, name="in_proj_b")
            u3 = u[Mp:].reshape(Bs, Ts, -1)
            tok_p, conv_p = _conv_prompt(u, conv_w[l // 2], batch=Bp, seq=Tp, dim=TOK)
            tok_s, conv_s = _conv_sample(u3, conv_w[l // 2], state_conv[l // 2], dim=TOK)
            q_col = 3 * TOK // MEM
        mem_p = _attn_prompt(u, kvs[l], batch=Bp, seq=Tp, n_mem=n_mem, q_col=q_col)
        mem_s = _attn_sample(u3, cache_k, cache_v, l, q_col=q_col)
        tok = jnp.concatenate([tok_p, tok_s.reshape(Ms, TOK).astype(BF16)], axis=0)
        mem = jnp.concatenate([mem_p, mem_s.reshape(Ms, MEM).astype(BF16)], axis=0)
        mix = _out_proj(tok, mem, w_out, l, bm=bm, bn=512)
        x1, xp, ti, tg = _ln_router(x, mix, ln1_g, ln1_b, router_w, router_b, l)
        x = _moe_layer(x1, xp, ti[:, :TOP_K], tg, w_gate_up, b_gate_up, w_down, b_down, ln2_g, ln2_b, l)

    y_prompt = x[:Mp].reshape(Bp, Tp, D)
    y_sample = x[Mp:].reshape(Bs, Ts, D)
    return (y_prompt, y_sample, new_mem_k, new_mem_v, hgrn_p[None], conv_p[None], hgrn_s[None], conv_s[None])
```

```python
import functools

import jax
import jax.numpy as jnp
from jax import lax
from jax.experimental import pallas as pl
from jax.experimental.pallas import tpu as pltpu

F32 = jnp.float32
BF16 = jnp.bfloat16

DEPTH = 2
HGRN_HEADS = 24
HGRN_DK = 128
HGRN_DV = 128
HGRN_CHUNK = 32
MEM_HEADS = 4
MEM_HEAD_DIM = 256
N_EXPERTS = 32
TOP_K = 4
CONV_W = 3
SWIGLU_LIMIT = 7.0
SWIGLU_ALPHA = 1.702
LN_EPS = 1e-5
RMS_EPS = 1e-6
DEEPNORM_ALPHA = (2.0 * DEPTH) ** 0.25

SUBLANES = 8
LANES = 128
VMEM_LIMIT = 56 * 1024 * 1024

MOE_GROUP_ROWS = 1280
MOE_SUB_ROWS = 64
MOE_SPARE_GROUPS = 4
COMBINE_TOKENS = 64


def _cparams(n_axes, vmem=VMEM_LIMIT):
    return pltpu.CompilerParams(dimension_semantics=("arbitrary",) * n_axes, vmem_limit_bytes=vmem)


def _sigmoid(x):
    return 1.0 / (1.0 + jnp.exp(-x))


def _mm_kernel(x_ref, w_ref, o_ref, wb_ref):
    @pl.when(pl.program_id(1) == 0)
    def _():
        wb_ref[...] = w_ref[...].astype(BF16)

    o_ref[...] = jnp.dot(x_ref[...], wb_ref[...], preferred_element_type=F32).astype(o_ref.dtype)


def _matmul(x, w3, layer, *, bm, bn, out_dtype, name):
    M, K = x.shape
    N = w3.shape[-1]
    assert M % bm == 0 and N % bn == 0
    return pl.pallas_call(
        _mm_kernel,
        grid=(N // bn, M // bm),
        in_specs=[
            pl.BlockSpec((bm, K), lambda j, i: (i, 0)),
            pl.BlockSpec((None, K, bn), lambda j, i: (layer, 0, j)),
        ],
        out_specs=pl.BlockSpec((bm, bn), lambda j, i: (i, j)),
        out_shape=jax.ShapeDtypeStruct((M, N), out_dtype),
        scratch_shapes=[pltpu.VMEM((K, bn), BF16)],
        compiler_params=_cparams(2),
        name=name,
    )(x, w3)


def _mm2_kernel(a_ref, b_ref, wa_ref, wb_ref, o_ref, wsa_ref, wsb_ref):
    @pl.when(pl.program_id(1) == 0)
    def _():
        wsa_ref[...] = wa_ref[...].astype(BF16)
        wsb_ref[...] = wb_ref[...].astype(BF16)

    acc = jnp.dot(a_ref[...], wsa_ref[...], preferred_element_type=F32)
    acc = acc + jnp.dot(b_ref[...], wsb_ref[...], preferred_element_type=F32)
    o_ref[...] = acc


def _out_proj(tok, mem, w_out, layer, *, bm, bn):
    M, Ka = tok.shape
    Kb = mem.shape[1]
    N = w_out.shape[-1]
    assert Ka % Kb == 0
    return pl.pallas_call(
        _mm2_kernel,
        grid=(N // bn, M // bm),
        in_specs=[
            pl.BlockSpec((bm, Ka), lambda j, i: (i, 0)),
            pl.BlockSpec((bm, Kb), lambda j, i: (i, 0)),
            pl.BlockSpec((None, Ka, bn), lambda j, i: (layer, 0, j)),
            pl.BlockSpec((None, Kb, bn), lambda j, i: (layer, Ka // Kb, j)),
        ],
        out_specs=pl.BlockSpec((bm, bn), lambda j, i: (i, j)),
        out_shape=jax.ShapeDtypeStruct((M, N), F32),
        scratch_shapes=[pltpu.VMEM((Ka, bn), BF16), pltpu.VMEM((Kb, bn), BF16)],
        compiler_params=_cparams(2),
        name="out_proj",
    )(tok, mem, w_out, w_out)


def _hgrn_chunk(uq, uf, ui, uo, lb, gn, st_in, st_out, kpad, gpad, vpad, *, rows, n_valid, state_kv):
    q = uq * _sigmoid(uq)
    forget = lb + (1.0 - lb) * _sigmoid(uf)
    kk = 1.0 - forget
    g = jnp.log(forget)
    v = ui
    if n_valid < rows:
        row = lax.broadcasted_iota(jnp.int32, (rows, LANES), 0)
        valid = row < n_valid
        kk = jnp.where(valid, kk, 0.0)
        g = jnp.where(valid, g, 0.0)
    G = g
    s = 1
    while s < n_valid:
        gpad[rows:, :] = G
        G = G + gpad[rows - s:2 * rows - s, :]
        s *= 2
    gpad[rows:, :] = G
    kpad[rows:, :] = kk
    vpad[rows:, :] = v
    parts = []
    for j in range(rows // SUBLANES):
        lo = j * SUBLANES
        if lo >= n_valid:
            parts.append(jnp.zeros((SUBLANES, LANES), F32))
            continue
        qj = q[lo:lo + SUBLANES, :]
        Gj = G[lo:lo + SUBLANES, :]
        acc = jnp.zeros((SUBLANES, LANES), F32)
        for d in range(min(lo + SUBLANES, n_valid)):
            a0 = rows + lo - d
            kd = kpad[a0:a0 + SUBLANES, :]
            Gd = gpad[a0:a0 + SUBLANES, :]
            vd = vpad[a0:a0 + SUBLANES, :]
            p = qj * kd * jnp.exp(Gj - Gd)
            acc = acc + jnp.sum(p, axis=-1, keepdims=True) * vd
        parts.append(acc)
    o = parts[0] if len(parts) == 1 else jnp.concatenate(parts, axis=0)
    st = st_in[...]
    qt = (q * jnp.exp(G)).astype(BF16)
    GL = G[n_valid - 1:n_valid, :]
    kt = (kk * jnp.exp(GL - G)).astype(BF16)
    contract0 = (((0,), (0,)), ((), ()))
    if state_kv:
        o = o + jnp.dot(qt, st.astype(BF16), preferred_element_type=F32)
        upd = lax.dot_general(kt, v.astype(BF16), contract0, preferred_element_type=F32)
        gl_rows = lax.dot_general(g, jnp.ones((rows, LANES), F32), contract0, preferred_element_type=F32,
                                  precision=lax.Precision.HIGHEST)
        st_out[...] = st * jnp.exp(gl_rows) + upd
    else:
        o = o + lax.dot_general(qt, st.astype(BF16), (((1,), (1,)), ((), ())), preferred_element_type=F32)
        upd = lax.dot_general(v.astype(BF16), kt, contract0, preferred_element_type=F32)
        st_out[...] = st * jnp.exp(GL) + upd
    ms = jnp.mean(o * o, axis=-1, keepdims=True)
    return o * lax.rsqrt(ms + RMS_EPS) * gn * (uo * _sigmoid(uo))


def _hgrn_prompt_kernel(uq_ref, uf_ref, ui_ref, uo_ref, lb_ref, gn_ref, tok_ref, sout_ref,
                        st_ref, kpad, gpad, vpad, *, tb, hb):
    c = pl.program_id(2)
    C = HGRN_CHUNK

    @pl.when(c == 0)
    def _():
        st_ref[...] = jnp.zeros_like(st_ref)
        kpad[...] = jnp.zeros_like(kpad)
        gpad[...] = jnp.zeros_like(gpad)
        vpad[...] = jnp.zeros_like(vpad)

    gn = gn_ref[...]

    def chunk(ci, carry):
        r0 = pl.multiple_of(ci * C, C)
        for i in range(hb):
            cols = slice(i * LANES, (i + 1) * LANES)
            out = _hgrn_chunk(uq_ref[pl.ds(r0, C), cols], uf_ref[pl.ds(r0, C), cols], ui_ref[pl.ds(r0, C), cols],
                              uo_ref[pl.ds(r0, C), cols], lb_ref[:, cols], gn, st_ref.at[i], st_ref.at[i],
                              kpad.at[i], gpad.at[i], vpad.at[i], rows=C, n_valid=C, state_kv=False)
            tok_ref[pl.ds(r0, C), cols] = out.astype(tok_ref.dtype)
        return carry

    lax.fori_loop(0, tb // C, chunk, 0)

    @pl.when(c == pl.num_programs(2) - 1)
    def _():
        for i in range(hb):
            sout_ref[i] = st_ref[i].T


def _hgrn_prompt(u, lb, gnorm, *, batch, seq, tb=256, hb=4):
    H = HGRN_HEADS
    nt = seq // tb
    nh = H // hb
    W = hb * LANES
    assert seq % tb == 0 and tb % HGRN_CHUNK == 0 and H % hb == 0
    col = lambda off: (lambda b, h, c: (b * nt + c, off + h))
    pad = pltpu.VMEM((hb, 2 * HGRN_CHUNK, LANES), F32)
    return pl.pallas_call(
        functools.partial(_hgrn_prompt_kernel, tb=tb, hb=hb),
        grid=(batch, nh, nt),
        in_specs=[
            pl.BlockSpec((tb, W), col(0)),
            pl.BlockSpec((tb, W), col(nh)),
            pl.BlockSpec((tb, W), col(2 * nh)),
            pl.BlockSpec((tb, W), col(3 * nh)),
            pl.BlockSpec((None, 1, W), lambda b, h, c: (h, 0, 0)),
            pl.BlockSpec((1, LANES), lambda b, h, c: (0, 0)),
        ],
        out_specs=[
            pl.BlockSpec((tb, W), lambda b, h, c: (b * nt + c, h)),
            pl.BlockSpec((None, hb, HGRN_DK, HGRN_DV), lambda b, h, c: (b, h, 0, 0)),
        ],
        out_shape=[
            jax.ShapeDtypeStruct((batch * seq, H * HGRN_DV), BF16),
            jax.ShapeDtypeStruct((batch, H, HGRN_DK, HGRN_DV), F32),
        ],
        scratch_shapes=[pltpu.VMEM((hb, HGRN_DV, HGRN_DK), F32), pad, pad, pad],
        compiler_params=_cparams(3),
        name="hgrn_prompt",
    )(u, u, u, u, lb.reshape(nh, 1, W), gnorm.reshape(1, LANES))


def _hgrn_sample_kernel(uq_ref, uf_ref, ui_ref, uo_ref, lb_ref, gn_ref, sin_ref, tok_ref, sout_ref,
                        kpad, gpad, vpad, inq, inf, ini, ino, *, bb, seq):
    R = SUBLANES

    @pl.when((pl.program_id(0) == 0) & (pl.program_id(1) == 0))
    def _():
        for r in (kpad, gpad, vpad, inq, inf, ini, ino):
            r[...] = jnp.zeros_like(r)

    lb = lb_ref[...]
    gn = gn_ref[...]
    for b in range(bb):
        inq[b, 0:seq, :] = uq_ref[b]
        inf[b, 0:seq, :] = uf_ref[b]
        ini[b, 0:seq, :] = ui_ref[b]
        ino[b, 0:seq, :] = uo_ref[b]
        out = _hgrn_chunk(inq[b], inf[b], ini[b], ino[b], lb, gn, sin_ref.at[b], sout_ref.at[b],
                          kpad.at[b], gpad.at[b], vpad.at[b], rows=R, n_valid=seq, state_kv=True)
        tok_ref[b] = out[0:seq, :]


def _hgrn_sample(u3, lb, gnorm, state, *, bb=8):
    B, T, _ = u3.shape
    H = HGRN_HEADS
    assert T <= SUBLANES and B % bb == 0
    col = lambda off: (lambda i, h: (i, 0, off + h))
    sspec = pl.BlockSpec((bb, None, HGRN_DK, HGRN_DV), lambda i, h: (i, h, 0, 0))
    pad = pltpu.VMEM((bb, 2 * SUBLANES, LANES), F32)
    row = pltpu.VMEM((bb, SUBLANES, LANES), F32)
    return pl.pallas_call(
        functools.partial(_hgrn_sample_kernel, bb=bb, seq=T),
        grid=(B // bb, H),
        in_specs=[
            pl.BlockSpec((bb, T, LANES), col(0)),
            pl.BlockSpec((bb, T, LANES), col(H)),
            pl.BlockSpec((bb, T, LANES), col(2 * H)),
            pl.BlockSpec((bb, T, LANES), col(3 * H)),
            pl.BlockSpec((None, 1, LANES), lambda i, h: (h, 0, 0)),
            pl.BlockSpec((1, LANES), lambda i, h: (0, 0)),
            sspec,
        ],
        out_specs=[pl.BlockSpec((bb, T, LANES), lambda i, h: (i, 0, h)), sspec],
        out_shape=[
            jax.ShapeDtypeStruct((B, T, H * HGRN_DV), F32),
            jax.ShapeDtypeStruct((B, H, HGRN_DK, HGRN_DV), F32),
        ],
        scratch_shapes=[pad, pad, pad, row, row, row, row],
        compiler_params=_cparams(2),
        name="hgrn_sample",
    )(u3, u3, u3, u3, lb.reshape(H, 1, LANES), gnorm.reshape(1, LANES), state)


def _conv_prompt_kernel(bg_ref, cg_ref, v_ref, w_ref, tok_ref, cout_ref, zpad, *, tb):
    t = pl.program_id(2)
    P = SUBLANES

    @pl.when(t == 0)
    def _():
        zpad[0:P, :] = jnp.zeros((P, zpad.shape[1]), F32)

    z = cg_ref[...] * v_ref[...]
    zpad[P:, :] = z
    w = w_ref[...]
    y = w[2:3, :] * z + w[1:2, :] * zpad[P - 1:P - 1 + tb, :] + w[0:1, :] * zpad[P - 2:P - 2 + tb, :]
    tok_ref[...] = (bg_ref[...] * y).astype(tok_ref.dtype)
    zpad[0:P, :] = zpad[tb:tb + P, :]

    @pl.when(t == pl.num_programs(2) - 1)
    def _():
        cout_ref[...] = zpad[P - (CONV_W - 1):P, :]


def _conv_prompt(u, conv_w, *, batch, seq, dim, tb=512, cb=512):
    nt = seq // tb
    nc = dim // cb
    assert seq % tb == 0 and dim % cb == 0
    col = lambda off: (lambda b, c, t: (b * nt + t, off + c))
    return pl.pallas_call(
        functools.partial(_conv_prompt_kernel, tb=tb),
        grid=(batch, nc, nt),
        in_specs=[
            pl.BlockSpec((tb, cb), col(0)),
            pl.BlockSpec((tb, cb), col(nc)),
            pl.BlockSpec((tb, cb), col(2 * nc)),
            pl.BlockSpec((CONV_W, cb), lambda b, c, t: (0, c)),
        ],
        out_specs=[
            pl.BlockSpec((tb, cb), lambda b, c, t: (b * nt + t, c)),
            pl.BlockSpec((None, CONV_W - 1, cb), lambda b, c, t: (b, 0, c)),
        ],
        out_shape=[
            jax.ShapeDtypeStruct((batch * seq, dim), BF16),
            jax.ShapeDtypeStruct((batch, CONV_W - 1, dim), F32),
        ],
        scratch_shapes=[pltpu.VMEM((tb + SUBLANES, cb), F32)],
        compiler_params=_cparams(3),
        name="conv_prompt",
    )(u, u, u, conv_w)


def _conv_sample_kernel(bg_ref, cg_ref, v_ref, w_ref, buf_ref, tok_ref, cout_ref, *, seq):
    w = w_ref[...]
    zp = [buf_ref[:, j, :] for j in range(CONV_W - 1)]
    zp += [cg_ref[:, t, :] * v_ref[:, t, :] for t in range(seq)]
    for t in range(seq):
        y = w[0:1, :] * zp[t]
        for j in range(1, CONV_W):
            y = y + w[j:j + 1, :] * zp[t + j]
        tok_ref[:, t, :] = bg_ref[:, t, :] * y
    for j in range(CONV_W - 1):
        cout_ref[:, j, :] = zp[seq + j]


def _conv_sample(u3, conv_w, buf, *, dim, bb=32, cb=512):
    B, T, _ = u3.shape
    nc = dim // cb
    assert B % bb == 0 and dim % cb == 0
    col = lambda off: (lambda i, c: (i, 0, off + c))
    bspec = pl.BlockSpec((bb, CONV_W - 1, cb), lambda i, c: (i, 0, c))
    return pl.pallas_call(
        functools.partial(_conv_sample_kernel, seq=T),
        grid=(B // bb, nc),
        in_specs=[
            pl.BlockSpec((bb, T, cb), col(0)),
            pl.BlockSpec((bb, T, cb), col(nc)),
            pl.BlockSpec((bb, T, cb), col(2 * nc)),
            pl.BlockSpec((CONV_W, cb), lambda i, c: (0, c)),
            bspec,
        ],
        out_specs=[pl.BlockSpec((bb, T, cb), lambda i, c: (i, 0, c)), bspec],
        out_shape=[
            jax.ShapeDtypeStruct((B, T, dim), F32),
            jax.ShapeDtypeStruct((B, CONV_W - 1, dim), F32),
        ],
        compiler_params=_cparams(2),
        name="conv_sample",
    )(u3, u3, u3, conv_w, buf)


def _attend(q, k, v):
    outs = []
    D = MEM_HEAD_DIM
    for h in range(MEM_HEADS):
        sl = slice(h * D, (h + 1) * D)
        s = lax.dot_general(q[:, sl].astype(BF16), k[:, sl].astype(BF16), (((1,), (1,)), ((), ())),
                            preferred_element_type=F32) * (D ** -0.5)
        m = jnp.max(s, axis=-1, keepdims=True)
        e = jnp.exp(s - m)
        p = e / jnp.sum(e, axis=-1, keepdims=True)
        outs.append(jnp.dot(p.astype(BF16), v[:, sl].astype(BF16), preferred_element_type=F32))
    return jnp.concatenate(outs, axis=-1)


def _attn_prompt_kernel(q_ref, k_ref, v_ref, o_ref):
    o_ref[...] = _attend(q_ref[...], k_ref[...], v_ref[...]).astype(o_ref.dtype)


def _attn_prompt(u, kv, *, batch, seq, n_mem, q_col, tq=512):
    W = MEM_HEADS * MEM_HEAD_DIM
    nt = seq // tq
    return pl.pallas_call(
        _attn_prompt_kernel,
        grid=(batch, nt),
        in_specs=[
            pl.BlockSpec((tq, W), lambda b, t: (b * nt + t, q_col)),
            pl.BlockSpec((n_mem, W), lambda b, t: (b, 0)),
            pl.BlockSpec((n_mem, W), lambda b, t: (b, 1)),
        ],
        out_specs=pl.BlockSpec((tq, W), lambda b, t: (b * nt + t, 0)),
        out_shape=jax.ShapeDtypeStruct((batch * seq, W), BF16),
        compiler_params=_cparams(2),
        name="attn_prompt",
    )(u, kv, kv)


def _attn_sample_kernel(q_ref, k_ref, v_ref, o_ref, qpad, *, bb, seq):
    @pl.when(pl.program_id(0) == 0)
    def _():
        qpad[...] = jnp.zeros_like(qpad)

    for b in range(bb):
        qpad[0:seq, :] = q_ref[b]
        o_ref[b] = _attend(qpad[...], k_ref[b], v_ref[b])[0:seq, :]


def _attn_sample(u3, mem_k, mem_v, layer, *, q_col, bb=4):
    B, T, _ = u3.shape
    n_mem, W = mem_k.shape[2], mem_k.shape[3]
    kspec = pl.BlockSpec((None, bb, n_mem, W), lambda i: (layer, i, 0, 0))
    return pl.pallas_call(
        functools.partial(_attn_sample_kernel, bb=bb, seq=T),
        grid=(B // bb,),
        in_specs=[pl.BlockSpec((bb, T, W), lambda i: (i, 0, q_col)), kspec, kspec],
        out_specs=pl.BlockSpec((bb, T, W), lambda i: (i, 0, 0)),
        out_shape=jax.ShapeDtypeStruct((B, T, W), F32),
        scratch_shapes=[pltpu.VMEM((SUBLANES, W), F32)],
        compiler_params=_cparams(1),
        name="attn_sample",
    )(u3, mem_k, mem_v)


def _layer_norm(h, g, b):
    mu = jnp.mean(h, axis=-1, keepdims=True)
    c = h - mu
    var = jnp.mean(c * c, axis=-1, keepdims=True)
    return c * lax.rsqrt(var + LN_EPS) * g + b


def _pack_bf16_pairs(y):
    w = y.shape[1] // 2
    bits = pltpu.bitcast(y.astype(BF16).astype(F32), jnp.uint32)
    return (bits[:, w:] & jnp.uint32(0xFFFF0000)) | (bits[:, :w] >> 16)


def _unpack_bf16_pairs(p):
    lo = pltpu.bitcast(p << 16, F32).astype(BF16)
    hi = pltpu.bitcast(p & jnp.uint32(0xFFFF0000), F32).astype(BF16)
    return lo, hi


def _ln_router_kernel(x_ref, mix_ref, g_ref, b_ref, rw_ref, rb_ref, y_ref, yp_ref, ti_ref, tg_ref):
    y = _layer_norm(DEEPNORM_ALPHA * x_ref[...] + mix_ref[...], g_ref[...], b_ref[...])
    y_ref[...] = y
    yp_ref[...] = _pack_bf16_pairs(y)
    logits = jnp.dot(y, rw_ref[...], preferred_element_type=F32, precision=lax.Precision.HIGHEST)
    logits = logits + rb_ref[...]
    n_e = logits.shape[-1]
    lane_e = lax.broadcasted_iota(jnp.int32, logits.shape, 1)
    lane = lax.broadcasted_iota(jnp.int32, ti_ref.shape, 1)
    vals = logits
    top_v, top_i = [], []
    for _ in range(TOP_K):
        m = jnp.max(vals, axis=-1, keepdims=True)
        idx = jnp.min(jnp.where(vals == m, lane_e, n_e), axis=-1, keepdims=True)
        top_v.append(m)
        top_i.append(idx)
        vals = jnp.where(lane_e == idx, -jnp.inf, vals)
    ex = [jnp.exp(tv - top_v[0]) for tv in top_v]
    den = ex[0]
    for e in ex[1:]:
        den = den + e
    ti = jnp.zeros(ti_ref.shape, jnp.int32)
    tg = jnp.zeros(tg_ref.shape, F32)
    for k in range(TOP_K):
        ti = jnp.where(lane == k, top_i[k], ti)
        tg = jnp.where(lane == k, ex[k] / den, tg)
    ti_ref[...] = ti
    tg_ref[...] = tg


def _ln_router(x, mix, ln_g, ln_b, router_w, router_b, layer, *, tm=256):
    M, D = x.shape
    E = router_w.shape[-1]
    row = pl.BlockSpec((tm, D), lambda i: (i, 0))
    half = pl.BlockSpec((tm, D // 2), lambda i: (i, 0))
    vec = pl.BlockSpec((None, 1, D), lambda i: (layer, 0, 0))
    wide = pl.BlockSpec((tm, LANES), lambda i: (i, 0))
    return pl.pallas_call(
        _ln_router_kernel,
        grid=(M // tm,),
        in_specs=[row, row, vec, vec,
                  pl.BlockSpec((None, D, E), lambda i: (layer, 0, 0)),
                  pl.BlockSpec((None, 1, E), lambda i: (layer, 0, 0))],
        out_specs=[row, half, wide, wide],
        out_shape=[jax.ShapeDtypeStruct((M, D), F32),
                   jax.ShapeDtypeStruct((M, D // 2), jnp.uint32),
                   jax.ShapeDtypeStruct((M, LANES), jnp.int32),
                   jax.ShapeDtypeStruct((M, LANES), F32)],
        compiler_params=_cparams(1),
        name="ln_router",
    )(x, mix, ln_g.reshape(-1, 1, D), ln_b.reshape(-1, 1, D), router_w, router_b.reshape(-1, 1, E))


def _row_copy(src_hbm, dst, sem, src_row, dst_row):
    return pltpu.make_async_copy(src_hbm.at[pl.ds(src_row, 1)], dst.at[pl.ds(dst_row, 1)], sem)


def _row_tiles(n_sub, tile_fn):
    sub = MOE_SUB_ROWS
    top = 1
    while 2 * top <= MOE_GROUP_ROWS // sub:
        top *= 2
    sizes = []
    while top >= 1:
        sizes.append(top)
        top //= 2
    for size in sizes:
        @pl.when((n_sub & size) != 0)
        def _(size=size):
            start = (n_sub & (-2 * size)) * sub
            tile_fn(pl.multiple_of(start, size * sub), size * sub)


def _zero_unused_rows(ref, n_sub, total_sub):
    sub = MOE_SUB_ROWS

    def fill(i, c):
        ref[pl.ds(pl.multiple_of(i * sub, sub), sub), :] = jnp.zeros((sub, ref.shape[1]), ref.dtype)
        return c

    lax.fori_loop(n_sub, total_sub, fill, 0)


def _moe_up_kernel(ge_ref, gn_ref, gm_ref, gr_ref, src_ref, srcn_ref, xp_hbm, wg_ref, wu_ref, bg_ref, bu_ref,
                   act_ref, xraw, xb, w_scr, sem, *, bf, chunk):
    g = pl.program_id(0)
    j = pl.program_id(1)
    n = gn_ref[g]
    sub = MOE_SUB_ROWS
    R, half = xraw.shape

    def row_fetch(idx_ref, r):
        return pltpu.make_async_copy(xp_hbm.at[pl.ds(idx_ref[0, r], 1)], xraw.at[pl.ds(r, 1)], sem)

    def drain_all():
        def drain(r, c):
            pltpu.make_async_copy(xp_hbm.at[pl.ds(0, 1)], xraw.at[pl.ds(0, 1)], sem).wait()
            return c

        lax.fori_loop(0, R, drain, 0)

    @pl.when((g == 0) & (j == 0))
    def _():
        def issue(r, c):
            row_fetch(src_ref, r).start()
            return c

        lax.fori_loop(0, R, issue, 0)

    @pl.when((n > 0) & (j == 0))
    def _():
        rows = gr_ref[g]
        drain_all()

        def unpack(t, c):
            r0 = pl.multiple_of(t * sub, sub)
            p = xraw[pl.ds(r0, sub), :]
            row = lax.broadcasted_iota(jnp.int32, p.shape, 0) + r0
            lo, hi = _unpack_bf16_pairs(jnp.where(row < rows, p, jnp.uint32(0)))
            xb[pl.ds(r0, sub), 0:half] = lo
            xb[pl.ds(r0, sub), half:2 * half] = hi
            return c

        lax.fori_loop(0, n, unpack, 0)

    @pl.when(n > 0)
    def _():
        for r in range(chunk):
            row_fetch(srcn_ref, j * chunk + r).start()
        w_scr[:, 0:bf] = wg_ref[...].astype(BF16)
        w_scr[:, bf:2 * bf] = wu_ref[...].astype(BF16)
        bg = bg_ref[...]
        bu = bu_ref[...]

        def tile(r0, rows):
            h = jnp.dot(xb[pl.ds(r0, rows), :], w_scr[...], preferred_element_type=F32)
            gate = jnp.minimum(h[:, 0:bf] + bg, SWIGLU_LIMIT)
            up = jnp.clip(h[:, bf:2 * bf] + bu, -SWIGLU_LIMIT, SWIGLU_LIMIT)
            act = (up + 1.0) * gate * _sigmoid(SWIGLU_ALPHA * gate)
            act_ref[pl.ds(r0, rows), :] = act.astype(act_ref.dtype)

        _row_tiles(n, tile)
        _zero_unused_rows(act_ref, n, act_ref.shape[0] // sub)

    @pl.when((g == pl.num_programs(0) - 1) & (j == pl.num_programs(1) - 1))
    def _():
        drain_all()


def _moe_down_kernel(ge_ref, gn_ref, gm_ref, gr_ref, a_ref, w_ref, b_ref, y_ref, w_scr):
    g = pl.program_id(0)
    n = gn_ref[g]

    @pl.when(n > 0)
    def _():
        w_scr[...] = w_ref[...].astype(BF16)
        bias = b_ref[...]

        def tile(r0, rows):
            y_ref[pl.ds(r0, rows), :] = jnp.dot(a_ref[pl.ds(r0, rows), :], w_scr[...],
                                                preferred_element_type=F32) + bias

        _row_tiles(n, tile)
        _zero_unused_rows(y_ref, n, y_ref.shape[0] // MOE_SUB_ROWS)


def _moe_experts(xp, src3, ge, gn, gm, gr, w_gate_up, b_gate_up, w_down, b_down, layer, *, bf=256, bn=512):
    R = MOE_GROUP_ROWS
    G = src3.shape[0]
    GR = G * R
    D = 2 * xp.shape[1]
    F = w_down.shape[2]
    E = w_down.shape[1]
    nj1, nj2 = F // bf, D // bn
    jsel = lambda g, j, gn_ref, last: jnp.where(gn_ref[g] > 0, j, last)
    assert R % nj1 == 0
    chunk = R // nj1

    act = pl.pallas_call(
        functools.partial(_moe_up_kernel, bf=bf, chunk=chunk),
        grid_spec=pltpu.PrefetchScalarGridSpec(
            num_scalar_prefetch=4,
            grid=(G, nj1),
            in_specs=[
                pl.BlockSpec((None, 1, R), lambda g, j, ge, gn, gm, gr: (gm[g], 0, 0), memory_space=pltpu.SMEM),
                pl.BlockSpec((None, 1, R), lambda g, j, ge, gn, gm, gr: (gm[jnp.minimum(g + 1, G - 1)], 0, 0),
                             memory_space=pltpu.SMEM),
                pl.BlockSpec(memory_space=pl.ANY),
                pl.BlockSpec((None, None, D, bf),
                             lambda g, j, ge, gn, gm, gr: (layer, ge[g], 0, jsel(g, j, gn, nj1 - 1))),
                pl.BlockSpec((None, None, D, bf),
                             lambda g, j, ge, gn, gm, gr: (layer, ge[g], 0, nj1 + jsel(g, j, gn, nj1 - 1))),
                pl.BlockSpec((None, None, 1, bf),
                             lambda g, j, ge, gn, gm, gr: (layer, ge[g], 0, jsel(g, j, gn, nj1 - 1))),
                pl.BlockSpec((None, None, 1, bf),
                             lambda g, j, ge, gn, gm, gr: (layer, ge[g], 0, nj1 + jsel(g, j, gn, nj1 - 1))),
            ],
            out_specs=pl.BlockSpec((R, bf), lambda g, j, ge, gn, gm, gr: (gm[g], jsel(g, j, gn, nj1 - 1))),
            scratch_shapes=[
                pltpu.VMEM((R, D // 2), jnp.uint32),
                pltpu.VMEM((R, D), BF16),
                pltpu.VMEM((D, 2 * bf), BF16),
                pltpu.SemaphoreType.DMA(()),
            ],
        ),
        out_shape=jax.ShapeDtypeStruct((GR, F), BF16),
        compiler_params=_cparams(2),
        name="moe_up",
    )(ge, gn, gm, gr, src3, src3, xp, w_gate_up, w_gate_up, b_gate_up.reshape(-1, E, 1, 2 * F),
      b_gate_up.reshape(-1, E, 1, 2 * F))

    return pl.pallas_call(
        _moe_down_kernel,
        grid_spec=pltpu.PrefetchScalarGridSpec(
            num_scalar_prefetch=4,
            grid=(G, nj2),
            in_specs=[
                pl.BlockSpec((R, F), lambda g, j, ge, gn, gm, gr: (gm[g], 0)),
                pl.BlockSpec((None, None, F, bn),
                             lambda g, j, ge, gn, gm, gr: (layer, ge[g], 0, jsel(g, j, gn, nj2 - 1))),
                pl.BlockSpec((None, None, 1, bn),
                             lambda g, j, ge, gn, gm, gr: (layer, ge[g], 0, jsel(g, j, gn, nj2 - 1))),
            ],
            out_specs=pl.BlockSpec((R, bn), lambda g, j, ge, gn, gm, gr: (gm[g], jsel(g, j, gn, nj2 - 1))),
            scratch_shapes=[pltpu.VMEM((F, bn), BF16)],
        ),
        out_shape=jax.ShapeDtypeStruct((GR, D), F32),
        compiler_params=_cparams(2),
        name="moe_down",
    )(ge, gn, gm, gr, act, w_down, b_down.reshape(-1, E, 1, D))


def _combine_ln_kernel(pos_ref, posn_ref, y_hbm, x_ref, tg_ref, g_ref, b_ref, o_ref, buf, sem, *, tm):
    i = pl.program_id(0)
    slot = i % 2

    def fetch(idx_ref, dst_slot, r, k):
        return _row_copy(y_hbm, buf.at[dst_slot, k], sem.at[dst_slot], idx_ref[0, r * TOP_K + k], r)

    def drain(dst_slot):
        def body(r, c):
            for k in range(TOP_K):
                _row_copy(y_hbm, buf.at[dst_slot, k], sem.at[dst_slot], 0, r).wait()
            return c

        lax.fori_loop(0, tm, body, 0)

    @pl.when(i == 0)
    def _():
        def issue(r, c):
            for k in range(TOP_K):
                fetch(pos_ref, 0, r, k).start()
            return c

        lax.fori_loop(0, tm, issue, 0)

    drain(slot)
    for r in range(tm):
        for k in range(TOP_K):
            fetch(posn_ref, 1 - slot, r, k).start(priority=(r * TOP_K + k) % 2)
    tg = tg_ref[...]
    ffn = tg[:, 0:1] * buf[slot, 0]
    for k in range(1, TOP_K):
        ffn = ffn + tg[:, k:k + 1] * buf[slot, k]
    o_ref[...] = _layer_norm(DEEPNORM_ALPHA * x_ref[...] + ffn, g_ref[...], b_ref[...])

    @pl.when(i == pl.num_programs(0) - 1)
    def _():
        drain(1 - slot)


def _combine_ln(ys, pos3, x, tg, ln_g, ln_b, layer, *, tm=COMBINE_TOKENS):
    M, D = x.shape
    n_tiles = M // tm
    row = pl.BlockSpec((tm, D), lambda i: (i, 0))
    vec = pl.BlockSpec((None, 1, D), lambda i: (layer, 0, 0))
    return pl.pallas_call(
        functools.partial(_combine_ln_kernel, tm=tm),
        grid=(n_tiles,),
        in_specs=[
            pl.BlockSpec((None, 1, tm * TOP_K), lambda i: (i, 0, 0), memory_space=pltpu.SMEM),
            pl.BlockSpec((None, 1, tm * TOP_K), lambda i: (jnp.minimum(i + 1, n_tiles - 1), 0, 0),
                         memory_space=pltpu.SMEM),
            pl.BlockSpec(memory_space=pl.ANY),
            row,
            pl.BlockSpec((tm, LANES), lambda i: (i, 0)),
            vec, vec,
        ],
        out_specs=row,
        out_shape=jax.ShapeDtypeStruct((M, D), F32),
        scratch_shapes=[pltpu.VMEM((2, TOP_K, tm, D), F32), pltpu.SemaphoreType.DMA((2,))],
        compiler_params=_cparams(1),
        name="moe_combine_ln",
    )(pos3, pos3, ys, x, tg, ln_g.reshape(-1, 1, D), ln_b.reshape(-1, 1, D))


def _routing_tables(top_i, n_groups):
    R, sub = MOE_GROUP_ROWS, MOE_SUB_ROWS
    M = top_i.shape[0]
    e_flat = top_i.reshape(-1)
    blk = LANES
    assert (M * TOP_K) % blk == 0
    onehot = (e_flat[:, None] == jnp.arange(N_EXPERTS, dtype=jnp.int32)[None, :]).astype(F32)
    onehot = onehot.reshape(-1, blk, N_EXPERTS)
    within = jnp.einsum("ts,bse->bte", jnp.tril(jnp.ones((blk, blk), F32)), onehot)
    totals = within[:, -1, :]
    before = jnp.cumsum(totals, axis=0) - totals
    rank = jnp.sum((within + before[:, None, :]) * onehot, axis=-1).reshape(-1).astype(jnp.int32) - 1
    count = (before[-1] + totals[-1]).astype(jnp.int32)
    groups_e = (count + R - 1) // R
    gend = jnp.cumsum(groups_e)
    gbase = gend - groups_e
    n_used = gend[-1]
    pos = (gbase[e_flat] + rank // R) * R + rank % R
    gid = jnp.arange(n_groups, dtype=jnp.int32)
    gm = jnp.minimum(gid, n_used - 1)
    ge = jnp.searchsorted(gend, gm, side="right").astype(jnp.int32)
    used = gid < n_used
    gr = jnp.where(used, jnp.clip(count[ge] - (gm - gbase[ge]) * R, 0, R), 0).astype(jnp.int32)
    gn = (gr + sub - 1) // sub
    tok = jnp.arange(M * TOP_K, dtype=jnp.int32) // TOP_K
    src = jnp.zeros((n_groups * R,), jnp.int32).at[pos].set(tok)
    return pos.astype(jnp.int32), src, ge, gn, gm.astype(jnp.int32), gr, n_used


def _moe_layer(x1, xp, top_i, tg, w_gate_up, b_gate_up, w_down, b_down, ln_g, ln_b, layer):
    M, D = x1.shape
    R = MOE_GROUP_ROWS
    n_full = N_EXPERTS + (M * TOP_K) // R
    n_common = min(n_full, N_EXPERTS + MOE_SPARE_GROUPS)
    pos, src, ge, gn, gm, gr, n_used = _routing_tables(top_i, n_full)
    pos3 = pos.reshape(-1, 1, COMBINE_TOKENS * TOP_K)

    def run(G):
        def f():
            ys = _moe_experts(xp, src[:G * R].reshape(G, 1, R), ge[:G], gn[:G], gm[:G], gr[:G],
                              w_gate_up, b_gate_up, w_down, b_down, layer)
            return _combine_ln(ys, pos3, x1, tg, ln_g, ln_b, layer)

        return f

    if n_common == n_full:
        return run(n_full)()
    return lax.cond(n_used <= n_common, run(n_common), run(n_full))


def kernel(x_prompt, x_sample, mem_prompt, cache_mem_k, cache_mem_v, state_hgrn, state_conv, hgrn_lb_logits, w_in_a, hgrn_gnorm, w_in_b, conv_w, w_mem_kv, w_out, ln1_g, ln1_b, router_w, router_b, w_gate_up, b_gate_up, w_down, b_down, ln2_g, ln2_b):
    Bp, Tp, D = x_prompt.shape
    Bs, Ts, _ = x_sample.shape
    n_mem = mem_prompt.shape[1]
    MEM = MEM_HEADS * MEM_HEAD_DIM
    TOK = HGRN_HEADS * HGRN_DV
    Mp, Ms = Bp * Tp, Bs * Ts
    M = Mp + Ms
    assert w_in_a.shape[-1] == 4 * TOK + MEM and w_in_b.shape[-1] == 3 * TOK + MEM
    assert w_gate_up.shape[1] == N_EXPERTS and w_out.shape[0] == DEPTH == 2

    memb = mem_prompt.reshape(Bp * n_mem, D).astype(BF16)
    kvs = [_matmul(memb, w_mem_kv, l, bm=Bp * n_mem, bn=512, out_dtype=F32, name="mem_kv") for l in range(DEPTH)]
    kv = jnp.stack(kvs).reshape(DEPTH, Bp, n_mem, 2 * MEM)
    new_mem_k = kv[..., :MEM].reshape(DEPTH, Bp, n_mem, MEM_HEADS, MEM_HEAD_DIM)
    new_mem_v = kv[..., MEM:].reshape(DEPTH, Bp, n_mem, MEM_HEADS, MEM_HEAD_DIM)
    cache_k = cache_mem_k.reshape(DEPTH, Bs, n_mem, MEM)
    cache_v = cache_mem_v.reshape(DEPTH, Bs, n_mem, MEM)

    lb_all = jnp.cumsum(jax.nn.softmax(hgrn_lb_logits.astype(F32), axis=0), axis=0)

    x = jnp.concatenate([x_prompt.reshape(Mp, D), x_sample.reshape(Ms, D)], axis=0)
    bm = M // 8
    for l in range(DEPTH):
        xb = x.astype(BF16)
        if l % 2 == 0:
            u = _matmul(xb, w_in_a, l // 2, bm=bm, bn=512, out_dtype=F32, name="in_proj_a")
            u3 = u[Mp:].reshape(Bs, Ts, -1)
            lb = lb_all[l]
            tok_p, hgrn_p = _hgrn_prompt(u, lb, hgrn_gnorm[l // 2], batch=Bp, seq=Tp)
            tok_s, hgrn_s = _hgrn_sample(u3, lb, hgrn_gnorm[l // 2], state_hgrn[l // 2])
            q_col = 4 * TOK // MEM
        else:
            u = _matmul(xb, w_in_b, l // 2, bm=bm, bn=512, out_dtype=F32, name="in_proj_b")
            u3 = u[Mp:].reshape(Bs, Ts, -1)
            tok_p, conv_p = _conv_prompt(u, conv_w[l // 2], batch=Bp, seq=Tp, dim=TOK)
            tok_s, conv_s = _conv_sample(u3, conv_w[l // 2], state_conv[l // 2], dim=TOK)
            q_col = 3 * TOK // MEM
        mem_p = _attn_prompt(u, kvs[l], batch=Bp, seq=Tp, n_mem=n_mem, q_col=q_col)
        mem_s = _attn_sample(u3, cache_k, cache_v, l, q_col=q_col)
        tok = jnp.concatenate([tok_p, tok_s.reshape(Ms, TOK).astype(BF16)], axis=0)
        mem = jnp.concatenate([mem_p, mem_s.reshape(Ms, MEM).astype(BF16)], axis=0)
        mix = _out_proj(tok, mem, w_out, l, bm=bm, bn=512)
        x1, xp, ti, tg = _ln_router(x, mix, ln1_g, ln1_b, router_w, router_b, l)
        x = _moe_layer(x1, xp, ti[:, :TOP_K], tg, w_gate_up, b_gate_up, w_down, b_down, ln2_g, ln2_b, l)

    y_prompt = x[:Mp].reshape(Bp, Tp, D)
    y_sample = x[Mp:].reshape(Bs, Ts, D)
    return (y_prompt, y_sample, new_mem_k, new_mem_v, hgrn_p[None], conv_p[None], hgrn_s[None], conv_s[None])
```
